```python
import jax, jax.numpy as jnp
from jax import lax
import numpy as np

D_MODEL = 1024
BATCH = 1
SEQ = 16384
DEPTH = 2
DEC_BATCH = 32
DEC_SEQ = 8
PAST_LEN = 16384
PAGE_SIZE = 128

N_HEADS = 8
HEAD_DIM = 64
ATTN_WIDTH = N_HEADS * HEAD_DIM
MOBA_BLOCK = 256
MOBA_TOPK = 3
SGU_GROUPS = 8
SGU_WIDTH = 512
SGU_GROUP_DIM = SGU_WIDTH // SGU_GROUPS
CHUNK = 128
D_FF = 4 * D_MODEL
IN_COLS = 3 * ATTN_WIDTH + 2 * SGU_WIDTH + 2 * D_MODEL
EPS = 1e-6
NEG = -1e30

kernel_name = 'hybrid_sgu_moba_decode_step'


def rmsnorm(x, g):
    x32 = x.astype(jnp.float32)
    y = x32 * lax.rsqrt(jnp.mean(x32 * x32, axis=-1, keepdims=True) + EPS)
    return (y * g.astype(jnp.float32)).astype(x.dtype)


def spatial_gating(u, z, g_sgu, w_s, b_s):
    B, T, _ = u.shape
    u = jax.nn.gelu(u)
    z = rmsnorm(jax.nn.gelu(z), g_sgu)
    n_chunks = -(-T // CHUNK)
    pad = n_chunks * CHUNK - T
    zp = jnp.pad(z, ((0, 0), (0, pad), (0, 0))).reshape(B, n_chunks, CHUNK, SGU_GROUPS, SGU_GROUP_DIM)
    causal = jnp.tril(jnp.ones((CHUNK, CHUNK), dtype=bool))
    w = jnp.where(causal[None], w_s, jnp.zeros((), w_s.dtype)).astype(zp.dtype)
    s = jnp.einsum('gpj,bnjgc->bnpgc', w, zp) + b_s.T.astype(zp.dtype)[None, None, :, :, None]
    s = s.reshape(B, n_chunks * CHUNK, SGU_WIDTH)[:, :T]
    return u * s, z


def moba_prompt(q, k, v):
    B, T, H, Dh = q.shape
    nb = -(-T // MOBA_BLOCK)
    pad = nb * MOBA_BLOCK - T

    def blocks(a):
        return jnp.pad(a, ((0, 0), (0, pad), (0, 0), (0, 0))).reshape(B, nb, MOBA_BLOCK, H, Dh)

    qb, kb, vb = blocks(q), blocks(k), blocks(v)
    k_mean = jnp.mean(kb.astype(jnp.float32), axis=2)
    n_sel = min(MOBA_TOPK, nb - 1)
    scale = Dh ** -0.5
    causal = jnp.tril(jnp.ones((MOBA_BLOCK, MOBA_BLOCK), dtype=bool))
    bidx = jnp.arange(B)[:, None, None, None]
    hidx = jnp.arange(H)[None, None, :, None]

    def one_block(i):
        qi = lax.dynamic_index_in_dim(qb, i, axis=1, keepdims=False)
        ki = lax.dynamic_index_in_dim(kb, i, axis=1, keepdims=False)
        vi = lax.dynamic_index_in_dim(vb, i, axis=1, keepdims=False)
        own = jnp.einsum('bqhd,bkhd->bqhk', qi, ki).astype(jnp.float32) * scale
        own = jnp.where(causal[None, :, None, :], own, NEG)
        if n_sel == 0:
            p = jax.nn.softmax(own, axis=-1)
            return jnp.einsum('bqhk,bkhd->bqhd', p.astype(vi.dtype), vi)
        gate = jnp.einsum('bqhd,bnhd->bqhn', qi.astype(jnp.float32), k_mean)
        gate = jnp.where(jnp.arange(nb) < i, gate, NEG)
        _, sel = lax.top_k(gate, n_sel)
        valid = sel < i
        ks = kb[bidx, sel, :, hidx]
        vs = vb[bidx, sel, :, hidx]
        sl = jnp.einsum('bqhd,bqhskd->bqhsk', qi, ks).astype(jnp.float32) * scale
        sl = jnp.where(valid[..., None], sl, NEG).reshape(B, MOBA_BLOCK, H, n_sel * MOBA_BLOCK)
        p = jax.nn.softmax(jnp.concatenate([sl, own], axis=-1), axis=-1)
        p_sel = p[..., :n_sel * MOBA_BLOCK].reshape(B, MOBA_BLOCK, H, n_sel, MOBA_BLOCK)
        p_own = p[..., n_sel * MOBA_BLOCK:]
        return (jnp.einsum('bqhsk,bqhskd->bqhd', p_sel.astype(vs.dtype), vs)
                + jnp.einsum('bqhk,bkhd->bqhd', p_own.astype(vi.dtype), vi))

    out = lax.map(one_block, jnp.arange(nb))
    out = jnp.transpose(out, (1, 0, 2, 3, 4)).reshape(B, nb * MOBA_BLOCK, H, Dh)
    return out[:, :T]


def moba_sample(q, k_new, v_new, cache_k, cache_v, layer, page_table):
    DB, T, H, Dh = q.shape
    n_pages = page_table.shape[1]
    past_len = n_pages * PAGE_SIZE
    nb_past = past_len // MOBA_BLOCK
    full = nb_past * MOBA_BLOCK
    tail = past_len - full
    ppb = MOBA_BLOCK // PAGE_SIZE
    scale = Dh ** -0.5
    k_past = cache_k[layer, page_table].reshape(DB, past_len, H, Dh)
    k_own = jnp.concatenate([k_past[:, full:], k_new], axis=1)
    if tail > 0:
        v_tail = cache_v[layer, page_table[:, full // PAGE_SIZE:]].reshape(DB, tail, H, Dh)
        v_own = jnp.concatenate([v_tail, v_new], axis=1)
    else:
        v_own = v_new
    own = jnp.einsum('bqhd,bkhd->bqhk', q, k_own).astype(jnp.float32) * scale
    own_mask = jnp.arange(tail + T)[None, :] <= (tail + jnp.arange(T))[:, None]
    own = jnp.where(own_mask[None, :, None, :], own, NEG)
    n_sel = min(MOBA_TOPK, nb_past)
    if n_sel == 0:
        p = jax.nn.softmax(own, axis=-1)
        return jnp.einsum('bqhk,bkhd->bqhd', p.astype(v_own.dtype), v_own)
    k_mean = jnp.mean(k_past[:, :full].reshape(DB, nb_past, MOBA_BLOCK, H, Dh).astype(jnp.float32), axis=2)
    gate = jnp.einsum('bqhd,bnhd->bqhn', q.astype(jnp.float32), k_mean)
    _, sel = lax.top_k(gate, n_sel)
    bidx = jnp.arange(DB)[:, None, None, None, None]
    hidx = jnp.arange(H)[None, None, :, None, None]
    pages = page_table[bidx, sel[..., None] * ppb + jnp.arange(ppb)]
    ks = cache_k[layer, pages, :, hidx].reshape(DB, T, H, n_sel * MOBA_BLOCK, Dh)
    vs = cache_v[layer, pages, :, hidx].reshape(DB, T, H, n_sel * MOBA_BLOCK, Dh)
    sl = jnp.einsum('bqhd,bqhkd->bqhk', q, ks).astype(jnp.float32) * scale
    p = jax.nn.softmax(jnp.concatenate([sl, own], axis=-1), axis=-1)
    p_sel = p[..., :n_sel * MOBA_BLOCK]
    p_own = p[..., n_sel * MOBA_BLOCK:]
    return (jnp.einsum('bqhk,bqhkd->bqhd', p_sel.astype(vs.dtype), vs)
            + jnp.einsum('bqhk,bkhd->bqhd', p_own.astype(v_own.dtype), v_own))


def trunk_layer(x, attend, g_pre_mix, w_in, g_sgu, w_s, b_s, w_pa, w_pb, w_out,
                g_post_mix, g_pre_ffn, w_up, w_down, g_post_ffn):
    B, T, _ = x.shape
    h = rmsnorm(x, g_pre_mix)
    offs = np.cumsum([ATTN_WIDTH, ATTN_WIDTH, ATTN_WIDTH, SGU_WIDTH, SGU_WIDTH, D_MODEL]).tolist()
    q, k, v, u, z, ga, gb = jnp.split(h @ w_in, offs, axis=-1)
    a_out, z_rows = spatial_gating(u, z, g_sgu, w_s, b_s)
    qh = q.reshape(B, T, N_HEADS, HEAD_DIM)
    kh = k.reshape(B, T, N_HEADS, HEAD_DIM)
    vh = v.reshape(B, T, N_HEADS, HEAD_DIM)
    b_out = attend(qh, kh, vh).reshape(B, T, ATTN_WIDTH)
    merged = jax.nn.sigmoid(ga) * (a_out @ w_pa) + jax.nn.sigmoid(gb) * (b_out @ w_pb)
    x = x + rmsnorm(merged @ w_out, g_post_mix)
    h2 = rmsnorm(x, g_pre_ffn)
    f = jnp.square(jax.nn.relu(h2 @ w_up)) @ w_down
    x = x + rmsnorm(f, g_post_ffn)
    return x, kh, vh, z_rows


def setup_inputs(seed: int = 0) -> dict:
    key = jax.random.key(seed)
    ks = jax.random.split(key, 20)
    n_pages = PAST_LEN // PAGE_SIZE
    used = DEC_BATCH * n_pages
    n_phys = used + used // 4 + 1
    f32 = jnp.float32

    def nrm(k, shape, scale):
        return jax.random.normal(k, shape, f32) * scale

    def gain(k, shape):
        return 1.0 + 0.05 * jax.random.normal(k, shape, f32)

    page_table = jax.random.permutation(ks[4], n_phys)[:used].reshape(DEC_BATCH, n_pages).astype(jnp.int32)
    return {
        'x_prompt': nrm(ks[0], (BATCH, SEQ, D_MODEL), 1.0),
        'x_sample': nrm(ks[1], (DEC_BATCH, DEC_SEQ, D_MODEL), 1.0),
        'cache_k': nrm(ks[2], (DEPTH, n_phys, PAGE_SIZE, N_HEADS, HEAD_DIM), 1.0),
        'cache_v': nrm(ks[3], (DEPTH, n_phys, PAGE_SIZE, N_HEADS, HEAD_DIM), 1.0),
        'page_table': page_table,
        'g_pre_mix': gain(ks[5], (DEPTH, D_MODEL)),
        'w_in': nrm(ks[6], (DEPTH, D_MODEL, IN_COLS), D_MODEL ** -0.5),
        'g_sgu': gain(ks[7], (DEPTH, SGU_WIDTH)),
        'w_s': nrm(ks[8], (DEPTH, SGU_GROUPS, CHUNK, CHUNK), CHUNK ** -0.5),
        'b_s': gain(ks[9], (DEPTH, SGU_GROUPS, CHUNK)),
        'w_pa': nrm(ks[10], (DEPTH, SGU_WIDTH, D_MODEL), SGU_WIDTH ** -0.5),
        'w_pb': nrm(ks[11], (DEPTH, ATTN_WIDTH, D_MODEL), ATTN_WIDTH ** -0.5),
        'w_out': nrm(ks[12], (DEPTH, D_MODEL, D_MODEL), D_MODEL ** -0.5),
        'g_post_mix': gain(ks[13], (DEPTH, D_MODEL)),
        'g_pre_ffn': gain(ks[14], (DEPTH, D_MODEL)),
        'w_up': nrm(ks[15], (DEPTH, D_MODEL, D_FF), D_MODEL ** -0.5),
        'w_down': nrm(ks[16], (DEPTH, D_FF, D_MODEL), D_FF ** -0.5),
        'g_post_ffn': gain(ks[17], (DEPTH, D_MODEL)),
    }


def reference(x_prompt, x_sample, cache_k, cache_v, page_table, g_pre_mix, w_in, g_sgu, w_s, b_s,
              w_pa, w_pb, w_out, g_post_mix, g_pre_ffn, w_up, w_down, g_post_ffn):
    xp, xs = x_prompt, x_sample
    kp_rows, vp_rows, ks_rows, vs_rows, zs_rows = [], [], [], [], []
    for l in range(DEPTH):
        params = (g_pre_mix[l], w_in[l], g_sgu[l], w_s[l], b_s[l], w_pa[l], w_pb[l], w_out[l],
                  g_post_mix[l], g_pre_ffn[l], w_up[l], w_down[l], g_post_ffn[l])
        xp, kp, vp, _ = trunk_layer(xp, moba_prompt, *params)

        def attend_sample(q, k, v, layer=l):
            return moba_sample(q, k, v, cache_k, cache_v, layer, page_table)

        xs, kn, vn, zn = trunk_layer(xs, attend_sample, *params)
        kp_rows.append(kp)
        vp_rows.append(vp)
        ks_rows.append(kn)
        vs_rows.append(vn)
        zs_rows.append(zn)
    return (xp, xs, jnp.stack(kp_rows), jnp.stack(vp_rows), jnp.stack(ks_rows), jnp.stack(vs_rows), jnp.stack(zs_rows))
```

```python
import functools

import jax
import jax.numpy as jnp
from jax import lax
from jax.experimental import pallas as pl
from jax.experimental.pallas import tpu as pltpu

N_HEADS = 8
HEAD_DIM = 64
ATTN_WIDTH = N_HEADS * HEAD_DIM
SGU_GROUPS = 8
SGU_WIDTH = 512
SGU_GROUP_DIM = SGU_WIDTH // SGU_GROUPS
CHUNK = 128
MOBA_BLOCK = 256
MOBA_TOPK = 3
PAGE_SIZE = 128
EPS = 1e-6
NEG = -1e30
REAL_SCORE_FLOOR = -1e29

LANES = 128
HALF = LANES // 2
N_PAIRS = N_HEADS // 2
VMEM_LIMIT_BYTES = 56 * 1024 * 1024

PROMPT_ROWS = 512
FFN_SPLIT = 4
PAGES_PER_STEP = 8

F32 = jnp.float32
BF16 = jnp.bfloat16


def _rmsnorm(x, g):
    return x * lax.rsqrt(jnp.mean(x * x, axis=-1, keepdims=True) + EPS) * g


def _dot(a, b):
    return jnp.dot(a, b, preferred_element_type=F32)


def _dot_nt(a, b):
    return lax.dot_general(a, b, (((1,), (1,)), ((), ())), preferred_element_type=F32)


def _const_spec(shape):
    zeros = (0,) * len(shape)
    return pl.BlockSpec(shape, lambda *_: zeros, pipeline_mode=pl.Buffered(1))


def _row_spec(rows, cols):
    return pl.BlockSpec((rows, cols), lambda i: (i, 0))


def _inproj_kernel(x_ref, gpre_ref, w_ref, gsgu_ref, ws_ref, bias_ref, *out_refs, chunk, emit_attn):
    if emit_attn:
        q_ref, k_ref, v_ref, a_ref, z_ref, kmean_ref, kaug_ref, vb_ref = out_refs
    else:
        q_ref, k_ref, v_ref, a_ref, z_ref = out_refs
    tm = x_ref.shape[0]
    h = _rmsnorm(x_ref[...], gpre_ref[...]).astype(BF16)
    y = _dot(h, w_ref[...])
    q = y[:, 0:ATTN_WIDTH]
    k = y[:, ATTN_WIDTH:2 * ATTN_WIDTH]
    v = y[:, 2 * ATTN_WIDTH:3 * ATTN_WIDTH]
    u = y[:, 3 * ATTN_WIDTH:3 * ATTN_WIDTH + SGU_WIDTH]
    z = y[:, 3 * ATTN_WIDTH + SGU_WIDTH:]
    q_ref[...] = q
    k_ref[...] = k
    v_ref[...] = v
    zn = _rmsnorm(jax.nn.gelu(z), gsgu_ref[...])
    z_ref[...] = zn

    n_chunks = tm // chunk
    low = lax.broadcasted_iota(jnp.int32, (chunk, LANES), 1) < HALF
    w_row = lax.broadcasted_iota(jnp.int32, (chunk, 2 * chunk), 0)
    w_col = lax.broadcasted_iota(jnp.int32, (chunk, 2 * chunk), 1)
    causal = jnp.where(w_col >= chunk, w_col - chunk, w_col) <= w_row
    pair_out = []
    for p in range(N_PAIRS):
        w = jnp.where(causal, ws_ref[p], 0.0).astype(BF16)
        cols = []
        for c in range(n_chunks):
            zc = zn[c * chunk:(c + 1) * chunk, p * LANES:(p + 1) * LANES]
            cols.append(jnp.concatenate([jnp.where(low, zc, 0.0), jnp.where(low, 0.0, zc)], axis=0))
        rhs = jnp.concatenate(cols, axis=1).astype(BF16)
        pair_out.append(_dot(w, rhs))
    bias = bias_ref[...]
    rows = []
    for c in range(n_chunks):
        rows.append(jnp.concatenate([po[:, c * LANES:(c + 1) * LANES] for po in pair_out], axis=1) + bias)
    s = jnp.concatenate(rows, axis=0)
    a_ref[...] = (jax.nn.gelu(u) * s).astype(BF16)

    if emit_attn:
        kmean_ref[0] = jnp.mean(k.reshape(tm // MOBA_BLOCK, MOBA_BLOCK, ATTN_WIDTH), axis=1)
        row_g = pl.program_id(0) * tm + lax.broadcasted_iota(jnp.int32, (tm, LANES), 0)
        blk = jnp.right_shift(row_g, MOBA_BLOCK.bit_length() - 1)
        lane = lax.broadcasted_iota(jnp.int32, (tm, LANES), 1)
        low_t = lane < HALF
        onehot_hi = jnp.where(lane - HALF == blk, 1.0, 0.0)
        onehot_lo = jnp.where(lane == blk, 1.0, 0.0)
        for p in range(N_PAIRS):
            kc = k[:, p * LANES:(p + 1) * LANES]
            kaug_ref[:, (2 * p) * LANES:(2 * p + 1) * LANES] = jnp.where(low_t, kc, onehot_hi).astype(BF16)
            kaug_ref[:, (2 * p + 1) * LANES:(2 * p + 2) * LANES] = jnp.where(low_t, onehot_lo, kc).astype(BF16)
        vb_ref[...] = v.astype(BF16)


def _inproj(x, g_pre, w_qkvuz, g_sgu, ws_pairs, bias, *, tm, chunk, emit_attn):
    rows, d_model = x.shape
    n_tiles = rows // tm
    out_shape = [jax.ShapeDtypeStruct((rows, ATTN_WIDTH), F32)] * 3 + [
        jax.ShapeDtypeStruct((rows, SGU_WIDTH), BF16),
        jax.ShapeDtypeStruct((rows, SGU_WIDTH), F32),
    ]
    out_specs = [_row_spec(tm, ATTN_WIDTH)] * 3 + [_row_spec(tm, SGU_WIDTH)] * 2
    if emit_attn:
        blocks_per_tile = tm // MOBA_BLOCK
        out_shape += [
            jax.ShapeDtypeStruct((n_tiles, blocks_per_tile, ATTN_WIDTH), F32),
            jax.ShapeDtypeStruct((rows, N_HEADS * LANES), BF16),
            jax.ShapeDtypeStruct((rows, ATTN_WIDTH), BF16),
        ]
        out_specs += [
            pl.BlockSpec((1, blocks_per_tile, ATTN_WIDTH), lambda i: (i, 0, 0)),
            _row_spec(tm, N_HEADS * LANES),
            _row_spec(tm, ATTN_WIDTH),
        ]
    return pl.pallas_call(
        functools.partial(_inproj_kernel, chunk=chunk, emit_attn=emit_attn),
        out_shape=out_shape,
        grid=(n_tiles,),
        in_specs=[
            _row_spec(tm, d_model),
            _const_spec(g_pre.shape),
            _const_spec(w_qkvuz.shape),
            _const_spec(g_sgu.shape),
            _const_spec(ws_pairs.shape),
            _const_spec(bias.shape),
        ],
        out_specs=out_specs,
        compiler_params=pltpu.CompilerParams(
            dimension_semantics=("arbitrary",), vmem_limit_bytes=VMEM_LIMIT_BYTES),
        name="inproj_attn" if emit_attn else "inproj",
    )(x, g_pre, w_qkvuz, g_sgu, ws_pairs, bias)


def _select_kernel(q_ref, gmat_ref, qaug_ref):
    i_f = pl.program_id(0).astype(F32)
    rows = q_ref.shape[0]
    lane = lax.broadcasted_iota(jnp.int32, (rows, LANES), 1)
    low = lane < HALF
    scale = HEAD_DIM ** -0.5
    for p in range(N_PAIRS):
        qc = q_ref[:, p * LANES:(p + 1) * LANES]
        for parity in range(2):
            h = 2 * p + parity
            gate_lanes = low if parity else jnp.logical_not(low)
            block_of_lane = (lane if parity else lane - HALF).astype(F32)
            g = jnp.dot(qc, gmat_ref[h], precision=lax.Precision.HIGHEST, preferred_element_type=F32)
            g = jnp.where(gate_lanes & (block_of_lane < i_f), g, NEG)
            keep = gate_lanes & (block_of_lane == i_f)
            for _ in range(MOBA_TOPK):
                m = jnp.max(g, axis=-1, keepdims=True)
                first = jnp.min(jnp.where(g == m, block_of_lane, 2.0 * LANES), axis=-1, keepdims=True)
                pick = gate_lanes & (block_of_lane == first)
                keep = keep | (pick & (g > REAL_SCORE_FLOOR))
                g = jnp.where(pick, NEG, g)
            mask_row = jnp.where(keep, 0.0, NEG)
            qaug_ref[:, h * LANES:(h + 1) * LANES] = jnp.where(gate_lanes, mask_row, qc * scale).astype(BF16)


def _select(q, gmat):
    rows = q.shape[0]
    return pl.pallas_call(
        _select_kernel,
        out_shape=jax.ShapeDtypeStruct((rows, N_HEADS * LANES), BF16),
        grid=(rows // MOBA_BLOCK,),
        in_specs=[_row_spec(MOBA_BLOCK, ATTN_WIDTH), _const_spec(gmat.shape)],
        out_specs=_row_spec(MOBA_BLOCK, N_HEADS * LANES),
        compiler_params=pltpu.CompilerParams(
            dimension_semantics=("arbitrary",), vmem_limit_bytes=VMEM_LIMIT_BYTES),
        name="moba_select",
    )(q, gmat)


def _moba_attn_kernel(qa_ref, ka_ref, vb_ref, o_ref):
    i = pl.program_id(1)
    blk = qa_ref.shape[0]
    causal = (lax.broadcasted_iota(jnp.int32, (blk, blk), 1)
              <= lax.broadcasted_iota(jnp.int32, (blk, blk), 0))
    own = pl.multiple_of(i * blk, blk)
    v_own = vb_ref[pl.ds(own, blk), :]
    outs = []
    for parity in range(2):
        lanes = slice(parity * LANES, (parity + 1) * LANES)
        qh = qa_ref[:, lanes]
        s = jnp.where(causal, _dot_nt(qh, ka_ref[pl.ds(own, blk), lanes]), NEG)
        m = jnp.max(s, axis=-1, keepdims=True)
        p = jnp.exp(s - m)
        l = jnp.sum(p, axis=-1, keepdims=True)
        acc = _dot(p.astype(BF16), v_own)

        def past_block(j, carry, qh=qh, lanes=lanes):
            m, l, acc = carry
            start = pl.multiple_of(j * blk, blk)
            s = _dot_nt(qh, ka_ref[pl.ds(start, blk), lanes])
            m_new = jnp.maximum(m, jnp.max(s, axis=-1, keepdims=True))
            alpha = jnp.exp(m - m_new)
            p = jnp.exp(s - m_new)
            l = alpha * l + jnp.sum(p, axis=-1, keepdims=True)
            acc = alpha * acc + _dot(p.astype(BF16), vb_ref[pl.ds(start, blk), :])
            return m_new, l, acc

        m, l, acc = lax.fori_loop(0, i, past_block, (m, l, acc))
        outs.append(acc / l)
    low = lax.broadcasted_iota(jnp.int32, (blk, LANES), 1) < HALF
    o_ref[...] = jnp.where(low, outs[0], outs[1]).astype(BF16)


def _moba_attn(qaug, kaug, vb):
    rows = qaug.shape[0]
    return pl.pallas_call(
        _moba_attn_kernel,
        out_shape=jax.ShapeDtypeStruct((rows, ATTN_WIDTH), BF16),
        grid=(N_PAIRS, rows // MOBA_BLOCK),
        in_specs=[
            pl.BlockSpec((MOBA_BLOCK, 2 * LANES), lambda c, i: (i, c)),
            pl.BlockSpec((rows, 2 * LANES), lambda c, i: (0, c)),
            pl.BlockSpec((rows, LANES), lambda c, i: (0, c)),
        ],
        out_specs=pl.BlockSpec((MOBA_BLOCK, LANES), lambda c, i: (i, c)),
        compiler_params=pltpu.CompilerParams(
            dimension_semantics=("arbitrary", "arbitrary"), vmem_limit_bytes=VMEM_LIMIT_BYTES),
        name="moba_prompt_attn",
    )(qaug, kaug, vb)


def _gate_matrices(kmean):
    n_blocks = kmean.shape[0]
    km = kmean.reshape(n_blocks, N_HEADS, HEAD_DIM).transpose(1, 2, 0)
    km = jnp.pad(km, ((0, 0), (0, 0), (0, HALF - n_blocks)))
    zeros = jnp.zeros_like(km)
    even = jnp.concatenate([jnp.concatenate([zeros, km], axis=2), jnp.zeros((N_HEADS, HALF, LANES), F32)], axis=1)
    odd = jnp.concatenate([jnp.zeros((N_HEADS, HALF, LANES), F32), jnp.concatenate([km, zeros], axis=2)], axis=1)
    is_odd = (jnp.arange(N_HEADS) % 2 == 1)[:, None, None]
    return jnp.where(is_odd, odd, even)


def _page_index(b, s, pt_ref, *, layer, slot):
    return (layer, pt_ref[b, s * PAGES_PER_STEP + slot % PAGES_PER_STEP], 0, 0, 0)


def _head_diagonal(x, rows_per_head):
    return jnp.concatenate(
        [x[h * rows_per_head:(h + 1) * rows_per_head, (h // 2) * LANES:(h // 2 + 1) * LANES]
         for h in range(N_HEADS)], axis=0)


def _sample_attn_kernel(pt_ref, qbd_ref, knew_ref, vnew_ref, ownbias_ref, *rest):
    del pt_ref
    k_refs = rest[:PAGES_PER_STEP]
    v_refs = rest[PAGES_PER_STEP:2 * PAGES_PER_STEP]
    o_ref, m_ref, l_ref, g_ref, part_ref = rest[2 * PAGES_PER_STEP:]
    s = pl.program_id(1)
    n_pages = part_ref.shape[0]
    rows = qbd_ref.shape[1]
    tokens = rows // N_HEADS
    lane = lax.broadcasted_iota(jnp.int32, (rows, LANES), 1)
    lane_f = lane.astype(F32)
    qb = (qbd_ref[0] * HEAD_DIM ** -0.5).astype(BF16)

    @pl.when(s == 0)
    def _():
        m_ref[...] = jnp.full(m_ref.shape, NEG, F32)
        l_ref[...] = jnp.zeros(l_ref.shape, F32)
        g_ref[...] = jnp.zeros(g_ref.shape, F32)

    for slot in range(PAGES_PER_STEP):
        page = s * PAGES_PER_STEP + slot
        kt = k_refs[slot][...].reshape(ATTN_WIDTH, PAGE_SIZE).astype(BF16)
        vt = v_refs[slot][...].reshape(ATTN_WIDTH, PAGE_SIZE).astype(BF16)
        sc = _dot(qb, kt)
        m = jnp.max(sc, axis=-1, keepdims=True)
        p = jnp.exp(sc - m)
        here = lane == page
        m_ref[...] = jnp.where(here, m, m_ref[...])
        l_ref[...] = jnp.where(here, jnp.sum(p, axis=-1, keepdims=True), l_ref[...])
        g_ref[...] = jnp.where(here, jnp.sum(sc, axis=-1, keepdims=True), g_ref[...])
        pb = p.astype(BF16)
        pair_rows = 2 * tokens
        part_ref[page] = jnp.concatenate(
            [_dot_nt(pb[c * pair_rows:(c + 1) * pair_rows], vt[c * LANES:(c + 1) * LANES])
             for c in range(N_PAIRS)], axis=0)

    @pl.when(s == pl.num_programs(1) - 1)
    def _():
        gate = g_ref[...]
        even = jnp.bitwise_and(lane, 1) == 0
        gate = gate + jnp.where(even, pltpu.roll(gate, n_pages - 1, 1), pltpu.roll(gate, 1, 1))
        block_first_lane = (lane - jnp.bitwise_and(lane, 1)).astype(F32)
        keep = jnp.zeros(gate.shape, jnp.bool_)
        for _ in range(MOBA_TOPK):
            gm = jnp.max(gate, axis=-1, keepdims=True)
            first = jnp.min(jnp.where(gate == gm, lane_f, 2.0 * LANES), axis=-1, keepdims=True)
            pick = block_first_lane == first
            keep = keep | pick
            gate = jnp.where(pick, NEG, gate)
        m_all = m_ref[...]
        s_own = _dot_nt(qb, knew_ref[0].astype(BF16)) + ownbias_ref[...]
        m_fin = jnp.maximum(jnp.max(jnp.where(keep, m_all, NEG), axis=-1, keepdims=True),
                            jnp.max(s_own, axis=-1, keepdims=True))
        p_own = jnp.exp(s_own - m_fin)
        w = jnp.where(keep, jnp.exp(m_all - m_fin), 0.0)
        l_fin = jnp.sum(p_own, axis=-1, keepdims=True) + jnp.sum(w * l_ref[...], axis=-1, keepdims=True)
        acc = _head_diagonal(_dot(p_own.astype(BF16), vnew_ref[0].astype(BF16)), tokens)
        for page in range(n_pages):
            acc = acc + w[:, page:page + 1] * part_ref[page]
        acc = acc / l_fin
        row = lax.broadcasted_iota(jnp.int32, (rows, LANES), 0)
        odd_head = jnp.bitwise_and(row, tokens) == tokens
        o_ref[0] = jnp.where(odd_head, pltpu.roll(acc, HALF, 1), acc)[:, :HEAD_DIM]


def _sample_attn(q, k_new, v_new, cache_kt, cache_vt, page_table, layer):
    n_seq, tokens, _ = q.shape
    n_pages = page_table.shape[1]
    rows = N_HEADS * tokens
    assert n_pages == LANES and MOBA_BLOCK == 2 * PAGE_SIZE
    assert tokens <= LANES and tokens & (tokens - 1) == 0
    assert n_pages * PAGE_SIZE // MOBA_BLOCK >= MOBA_TOPK
    q4 = q.reshape(n_seq, tokens, N_HEADS, HEAD_DIM).transpose(0, 2, 1, 3)
    qbd = jnp.einsum("bhtd,hg->bhtgd", q4, jnp.eye(N_HEADS, dtype=F32)).reshape(n_seq, rows, ATTN_WIDTH)
    pad = ((0, 0), (0, LANES - tokens), (0, 0))
    k_new = jnp.pad(k_new, pad)
    v_new = jnp.pad(v_new, pad)
    t_q = jnp.arange(rows)[:, None] % tokens
    ownbias = jnp.where(jnp.arange(LANES)[None, :] <= t_q, 0.0, NEG).astype(F32)
    page_block = (None, None, N_HEADS, HEAD_DIM, PAGE_SIZE)
    page_specs = [pl.BlockSpec(page_block, functools.partial(_page_index, layer=layer, slot=slot))
                  for slot in range(2 * PAGES_PER_STEP)]
    grid_spec = pltpu.PrefetchScalarGridSpec(
        num_scalar_prefetch=1,
        grid=(n_seq, n_pages // PAGES_PER_STEP),
        in_specs=[
            pl.BlockSpec((1, rows, ATTN_WIDTH), lambda b, s, pt: (b, 0, 0)),
            pl.BlockSpec((1, LANES, ATTN_WIDTH), lambda b, s, pt: (b, 0, 0)),
            pl.BlockSpec((1, LANES, ATTN_WIDTH), lambda b, s, pt: (b, 0, 0)),
            pl.BlockSpec(ownbias.shape, lambda b, s, pt: (0, 0)),
        ] + page_specs,
        out_specs=pl.BlockSpec((1, rows, HEAD_DIM), lambda b, s, pt: (b, 0, 0)),
        scratch_shapes=[
            pltpu.VMEM((rows, n_pages), F32),
            pltpu.VMEM((rows, n_pages), F32),
            pltpu.VMEM((rows, n_pages), F32),
            pltpu.VMEM((n_pages, rows, LANES), F32),
        ],
    )
    return pl.pallas_call(
        _sample_attn_kernel,
        out_shape=jax.ShapeDtypeStruct((n_seq, rows, HEAD_DIM), F32),
        grid_spec=grid_spec,
        compiler_params=pltpu.CompilerParams(
            dimension_semantics=("arbitrary", "arbitrary"), vmem_limit_bytes=VMEM_LIMIT_BYTES),
        name="moba_sample_attn",
    )(page_table, qbd, k_new, v_new, ownbias,
      *([cache_kt] * PAGES_PER_STEP), *([cache_vt] * PAGES_PER_STEP))


def _post_kernel(x_ref, a_ref, b_ref, gpre_ref, wg_ref, wpa_ref, wpb_ref, wout_ref,
                 gpm_ref, gpf_ref, wup_ref, wdown_ref, gff_ref, o_ref):
    x = x_ref[...]
    d_model = x.shape[1]
    h = _rmsnorm(x, gpre_ref[...]).astype(BF16)
    gates = jax.nn.sigmoid(_dot(h, wg_ref[...]))
    merged = (gates[:, :d_model] * _dot(a_ref[...], wpa_ref[...])
              + gates[:, d_model:] * _dot(b_ref[...], wpb_ref[...]))
    x = x + _rmsnorm(_dot(merged.astype(BF16), wout_ref[...]), gpm_ref[...])
    h2 = _rmsnorm(x, gpf_ref[...]).astype(BF16)
    slab = wup_ref.shape[1] // FFN_SPLIT
    f = jnp.zeros(x.shape, F32)
    for c in range(FFN_SPLIT):
        up = _dot(h2, wup_ref[:, c * slab:(c + 1) * slab])
        f = f + _dot(jnp.square(jnp.maximum(up, 0.0)).astype(BF16), wdown_ref[c * slab:(c + 1) * slab, :])
    o_ref[...] = x + _rmsnorm(f, gff_ref[...])


def _post(x, a, b, g_pre, w_g, w_pa, w_pb, w_out, g_pm, g_pf, w_up, w_down, g_ff, *, tm):
    rows, d_model = x.shape
    consts = (g_pre, w_g, w_pa, w_pb, w_out, g_pm, g_pf, w_up, w_down, g_ff)
    return pl.pallas_call(
        _post_kernel,
        out_shape=jax.ShapeDtypeStruct((rows, d_model), F32),
        grid=(rows // tm,),
        in_specs=[_row_spec(tm, d_model), _row_spec(tm, SGU_WIDTH), _row_spec(tm, ATTN_WIDTH)]
        + [_const_spec(c.shape) for c in consts],
        out_specs=_row_spec(tm, d_model),
        compiler_params=pltpu.CompilerParams(
            dimension_semantics=("arbitrary",), vmem_limit_bytes=VMEM_LIMIT_BYTES),
        name="merge_mlp",
    )(x, a, b, *consts)


def _pair_weights(w):
    g, p, j = w.shape
    return w.reshape(g // 2, 2, p, j).transpose(0, 2, 1, 3).reshape(g // 2, p, 2 * j)


def kernel(x_prompt, x_sample, cache_k, cache_v, page_table, g_pre_mix, w_in, g_sgu, w_s, b_s, w_pa, w_pb,
           w_out, g_post_mix, g_pre_ffn, w_up, w_down, g_post_ffn):
    depth = w_in.shape[0]
    batch, seq, d_model = x_prompt.shape
    dec_batch, dec_seq, _ = x_sample.shape
    assert batch == 1 and seq % PROMPT_ROWS == 0 and seq // MOBA_BLOCK <= HALF
    assert dec_seq <= CHUNK and SGU_GROUP_DIM == HALF and HEAD_DIM == HALF
    n_qkvuz = 3 * ATTN_WIDTH + 2 * SGU_WIDTH
    dec_rows = dec_batch * dec_seq

    cache_kt = cache_k.transpose(0, 1, 3, 4, 2)
    cache_vt = cache_v.transpose(0, 1, 3, 4, 2)
    xp = x_prompt.reshape(seq, d_model)
    xs = x_sample.reshape(dec_rows, d_model)
    kp_rows, vp_rows, ks_rows, vs_rows, zs_rows = [], [], [], [], []
    for l in range(depth):
        row = lambda g: g[l][None, :]
        w_qkvuz = w_in[l][:, :n_qkvuz].astype(BF16)
        w_gates = w_in[l][:, n_qkvuz:].astype(BF16)
        post_w = (row(g_pre_mix), w_gates, w_pa[l].astype(BF16), w_pb[l].astype(BF16), w_out[l].astype(BF16),
                  row(g_post_mix), row(g_pre_ffn), w_up[l].astype(BF16), w_down[l].astype(BF16), row(g_post_ffn))

        bias_p = jnp.repeat(b_s[l].T, SGU_GROUP_DIM, axis=1)
        q, k, v, a, _, kmean, kaug, vb = _inproj(
            xp, row(g_pre_mix), w_qkvuz, row(g_sgu), _pair_weights(w_s[l]), bias_p,
            tm=PROMPT_ROWS, chunk=CHUNK, emit_attn=True)
        qaug = _select(q, _gate_matrices(kmean.reshape(seq // MOBA_BLOCK, ATTN_WIDTH)))
        b_attn = _moba_attn(qaug, kaug, vb)
        xp = _post(xp, a, b_attn, *post_w, tm=PROMPT_ROWS)
        kp_rows.append(k.reshape(batch, seq, N_HEADS, HEAD_DIM))
        vp_rows.append(v.reshape(batch, seq, N_HEADS, HEAD_DIM))

        w_blockdiag = jnp.einsum("bc,gpj->gbpcj", jnp.eye(dec_batch, dtype=F32),
                                 w_s[l][:, :dec_seq, :dec_seq]).reshape(SGU_GROUPS, dec_rows, dec_rows)
        bias_s = jnp.tile(bias_p[:dec_seq], (dec_batch, 1))
        q, k, v, a, z = _inproj(
            xs, row(g_pre_mix), w_qkvuz, row(g_sgu), _pair_weights(w_blockdiag), bias_s,
            tm=dec_rows, chunk=dec_rows, emit_attn=False)
        o = _sample_attn(q.reshape(dec_batch, dec_seq, ATTN_WIDTH), k.reshape(dec_batch, dec_seq, ATTN_WIDTH),
                         v.reshape(dec_batch, dec_seq, ATTN_WIDTH), cache_kt, cache_vt, page_table, l)
        b_attn = o.reshape(dec_batch, N_HEADS, dec_seq, HEAD_DIM).transpose(0, 2, 1, 3).reshape(
            dec_rows, ATTN_WIDTH).astype(BF16)
        xs = _post(xs, a, b_attn, *post_w, tm=dec_rows)
        ks_rows.append(k.reshape(dec_batch, dec_seq, N_HEADS, HEAD_DIM))
        vs_rows.append(v.reshape(dec_batch, dec_seq, N_HEADS, HEAD_DIM))
        zs_rows.append(z.reshape(dec_batch, dec_seq, SGU_WIDTH))

    return (xp.reshape(batch, seq, d_model), xs.reshape(dec_batch, dec_seq, d_model),
            jnp.stack(kp_rows), jnp.stack(vp_rows), jnp.stack(ks_rows), jnp.stack(vs_rows), jnp.stack(zs_rows))
```

```python
import functools

import jax
import jax.numpy as jnp
from jax import lax
from jax.experimental import pallas as pl
from jax.experimental.pallas import tpu as pltpu

N_HEADS = 8
HEAD_DIM = 64
ATTN_WIDTH = N_HEADS * HEAD_DIM
SGU_GROUPS = 8
SGU_WIDTH = 512
SGU_GROUP_DIM = SGU_WIDTH // SGU_GROUPS
CHUNK = 128
MOBA_BLOCK = 256
MOBA_TOPK = 3
PAGE_SIZE = 128
EPS = 1e-6
NEG = -1e30
REAL_SCORE_FLOOR = -1e29

LANES = 128
HALF = LANES // 2
N_PAIRS = N_HEADS // 2
VMEM_LIMIT_BYTES = 56 * 1024 * 1024

PROMPT_ROWS = 512
FFN_SPLIT = 4
KV_TILE = 1024
PAGES_PER_STEP = 8

F32 = jnp.float32
BF16 = jnp.bfloat16


def _rmsnorm(x, g):
    return x * lax.rsqrt(jnp.mean(x * x, axis=-1, keepdims=True) + EPS) * g


def _dot(a, b):
    return jnp.dot(a, b, preferred_element_type=F32)


def _dot_nt(a, b):
    return lax.dot_general(a, b, (((1,), (1,)), ((), ())), preferred_element_type=F32)


def _const_spec(shape):
    zeros = (0,) * len(shape)
    return pl.BlockSpec(shape, lambda *_: zeros, pipeline_mode=pl.Buffered(1))


def _row_spec(rows, cols):
    return pl.BlockSpec((rows, cols), lambda i: (i, 0))


def _inproj_kernel(x_ref, gpre_ref, w_ref, gsgu_ref, ws_ref, bias_ref, *out_refs, chunk, emit_attn):
    if emit_attn:
        q_ref, a_ref, kmean_ref, kaug_ref, kt_ref, vt_ref, vtb_ref = out_refs
    else:
        q_ref, a_ref, k_ref, v_ref, z_ref = out_refs
    tm = x_ref.shape[0]
    h = _rmsnorm(x_ref[...], gpre_ref[...]).astype(BF16)
    y = _dot(h, w_ref[...])
    q = y[:, 0:ATTN_WIDTH]
    k = y[:, ATTN_WIDTH:2 * ATTN_WIDTH]
    v = y[:, 2 * ATTN_WIDTH:3 * ATTN_WIDTH]
    u = y[:, 3 * ATTN_WIDTH:3 * ATTN_WIDTH + SGU_WIDTH]
    z = y[:, 3 * ATTN_WIDTH + SGU_WIDTH:]
    q_ref[...] = q
    zn = _rmsnorm(jax.nn.gelu(z), gsgu_ref[...])
    if not emit_attn:
        k_ref[...] = k
        v_ref[...] = v
        z_ref[...] = zn

    n_chunks = tm // chunk
    low = lax.broadcasted_iota(jnp.int32, (chunk, LANES), 1) < HALF
    w_row = lax.broadcasted_iota(jnp.int32, (chunk, 2 * chunk), 0)
    w_col = lax.broadcasted_iota(jnp.int32, (chunk, 2 * chunk), 1)
    causal = jnp.where(w_col >= chunk, w_col - chunk, w_col) <= w_row
    pair_out = []
    for p in range(N_PAIRS):
        w = jnp.where(causal, ws_ref[p], 0.0).astype(BF16)
        cols = []
        for c in range(n_chunks):
            zc = zn[c * chunk:(c + 1) * chunk, p * LANES:(p + 1) * LANES]
            cols.append(jnp.concatenate([jnp.where(low, zc, 0.0), jnp.where(low, 0.0, zc)], axis=0))
        rhs = jnp.concatenate(cols, axis=1).astype(BF16)
        pair_out.append(_dot(w, rhs))
    bias = bias_ref[...]
    rows = []
    for c in range(n_chunks):
        rows.append(jnp.concatenate([po[:, c * LANES:(c + 1) * LANES] for po in pair_out], axis=1) + bias)
    s = jnp.concatenate(rows, axis=0)
    a_ref[...] = (jax.nn.gelu(u) * s).astype(BF16)

    if emit_attn:
        kmean_ref[0] = jnp.mean(k.reshape(tm // MOBA_BLOCK, MOBA_BLOCK, ATTN_WIDTH), axis=1)
        row_g = pl.program_id(0) * tm + lax.broadcasted_iota(jnp.int32, (tm, LANES), 0)
        blk = jnp.right_shift(row_g, MOBA_BLOCK.bit_length() - 1)
        lane = lax.broadcasted_iota(jnp.int32, (tm, LANES), 1)
        low_t = lane < HALF
        onehot_hi = jnp.where(lane - HALF == blk, 1.0, 0.0)
        onehot_lo = jnp.where(lane == blk, 1.0, 0.0)
        for p in range(N_PAIRS):
            kc = k[:, p * LANES:(p + 1) * LANES]
            kaug_ref[:, (2 * p) * LANES:(2 * p + 1) * LANES] = jnp.where(low_t, kc, onehot_hi).astype(BF16)
            kaug_ref[:, (2 * p + 1) * LANES:(2 * p + 2) * LANES] = jnp.where(low_t, onehot_lo, kc).astype(BF16)
        kt_ref[...] = k.T
        vt = v.T
        vt_ref[...] = vt
        vtb_ref[0] = vt.astype(BF16)


def _inproj(x, g_pre, w_qkvuz, g_sgu, ws_pairs, bias, *, tm, chunk, emit_attn):
    rows, d_model = x.shape
    n_tiles = rows // tm
    out_shape = [jax.ShapeDtypeStruct((rows, ATTN_WIDTH), F32), jax.ShapeDtypeStruct((rows, SGU_WIDTH), BF16)]
    out_specs = [_row_spec(tm, ATTN_WIDTH), _row_spec(tm, SGU_WIDTH)]
    if emit_attn:
        blocks_per_tile = tm // MOBA_BLOCK
        out_shape += [
            jax.ShapeDtypeStruct((n_tiles, blocks_per_tile, ATTN_WIDTH), F32),
            jax.ShapeDtypeStruct((rows, N_HEADS * LANES), BF16),
            jax.ShapeDtypeStruct((ATTN_WIDTH, rows), F32),
            jax.ShapeDtypeStruct((ATTN_WIDTH, rows), F32),
            jax.ShapeDtypeStruct((rows // KV_TILE, ATTN_WIDTH, KV_TILE), BF16),
        ]
        tiles_per_kv = KV_TILE // tm
        out_specs += [
            pl.BlockSpec((1, blocks_per_tile, ATTN_WIDTH), lambda i: (i, 0, 0)),
            _row_spec(tm, N_HEADS * LANES),
            pl.BlockSpec((ATTN_WIDTH, tm), lambda i: (0, i)),
            pl.BlockSpec((ATTN_WIDTH, tm), lambda i: (0, i)),
            pl.BlockSpec((1, ATTN_WIDTH, tm), lambda i: (i // tiles_per_kv, 0, i % tiles_per_kv)),
        ]
    else:
        out_shape += [jax.ShapeDtypeStruct((rows, ATTN_WIDTH), F32)] * 2 + [
            jax.ShapeDtypeStruct((rows, SGU_WIDTH), F32)]
        out_specs += [_row_spec(tm, ATTN_WIDTH)] * 2 + [_row_spec(tm, SGU_WIDTH)]
    return pl.pallas_call(
        functools.partial(_inproj_kernel, chunk=chunk, emit_attn=emit_attn),
        out_shape=out_shape,
        grid=(n_tiles,),
        in_specs=[
            _row_spec(tm, d_model),
            _const_spec(g_pre.shape),
            _const_spec(w_qkvuz.shape),
            _const_spec(g_sgu.shape),
            _const_spec(ws_pairs.shape),
            _const_spec(bias.shape),
        ],
        out_specs=out_specs,
        compiler_params=pltpu.CompilerParams(
            dimension_semantics=("arbitrary",), vmem_limit_bytes=VMEM_LIMIT_BYTES),
        name="inproj_attn" if emit_attn else "inproj",
    )(x, g_pre, w_qkvuz, g_sgu, ws_pairs, bias)


def _select_kernel(q_ref, gmat_ref, qaug_ref):
    i_f = pl.program_id(0).astype(F32)
    rows = q_ref.shape[0]
    lane = lax.broadcasted_iota(jnp.int32, (rows, LANES), 1)
    low = lane < HALF
    scale = HEAD_DIM ** -0.5
    for p in range(N_PAIRS):
        qc = q_ref[:, p * LANES:(p + 1) * LANES]
        for parity in range(2):
            h = 2 * p + parity
            gate_lanes = low if parity else jnp.logical_not(low)
            block_of_lane = (lane if parity else lane - HALF).astype(F32)
            g = jnp.dot(qc, gmat_ref[h], precision=lax.Precision.HIGHEST, preferred_element_type=F32)
            g = jnp.where(gate_lanes & (block_of_lane < i_f), g, NEG)
            keep = jnp.zeros(g.shape, jnp.bool_)
            for _ in range(MOBA_TOPK):
                m = jnp.max(g, axis=-1, keepdims=True)
                first = jnp.min(jnp.where(g == m, block_of_lane, 2.0 * LANES), axis=-1, keepdims=True)
                pick = gate_lanes & (block_of_lane == first)
                keep = keep | (pick & (g > REAL_SCORE_FLOOR))
                g = jnp.where(pick, NEG, g)
            mask_row = jnp.where(keep, 0.0, NEG)
            qaug_ref[:, h * LANES:(h + 1) * LANES] = jnp.where(gate_lanes, mask_row, qc * scale).astype(BF16)


def _select(q, gmat):
    rows = q.shape[0]
    return pl.pallas_call(
        _select_kernel,
        out_shape=jax.ShapeDtypeStruct((rows, N_HEADS * LANES), BF16),
        grid=(rows // MOBA_BLOCK,),
        in_specs=[_row_spec(MOBA_BLOCK, ATTN_WIDTH), _const_spec(gmat.shape)],
        out_specs=_row_spec(MOBA_BLOCK, N_HEADS * LANES),
        compiler_params=pltpu.CompilerParams(
            dimension_semantics=("arbitrary",), vmem_limit_bytes=VMEM_LIMIT_BYTES),
        name="moba_select",
    )(q, gmat)


def _moba_attn_kernel(qa_ref, ka_ref, vt_ref, vt_own_ref, o_ref):
    i = pl.program_id(1)
    blk = qa_ref.shape[0]
    tile = vt_ref.shape[2]
    blocks_per_tile = tile // blk
    lane = lax.broadcasted_iota(jnp.int32, (blk, LANES), 1)
    qs = [qa_ref[:, p * LANES:(p + 1) * LANES] for p in range(2)]

    def scores(k_rows, qs):
        return [_dot_nt(k_rows[:, p * LANES:(p + 1) * LANES], qs[p]) for p in range(2)]

    def absorb(state, s_pair, vt_cols):
        new = []
        for p in range(2):
            m, l, acc = state[p]
            s = s_pair[p]
            m_new = jnp.maximum(m, jnp.max(s, axis=0, keepdims=True))
            alpha = jnp.exp(m - m_new)
            e = jnp.exp(s - m_new)
            l = alpha * l + jnp.sum(e, axis=0, keepdims=True)
            acc = alpha * acc + _dot(vt_cols[p * HALF:(p + 1) * HALF], e.astype(BF16))
            new.append((m_new, l, acc))
        return tuple(new)

    q_plain = [jnp.where(lane < HALF if p == 0 else lane >= HALF, qs[p], jnp.zeros_like(qs[p]))
               for p in range(2)]
    key_le_query = (lax.broadcasted_iota(jnp.int32, (blk, blk), 0)
                    <= lax.broadcasted_iota(jnp.int32, (blk, blk), 1))
    init = tuple((jnp.full((1, blk), NEG, F32), jnp.zeros((1, blk), F32), jnp.zeros((HALF, blk), F32))
                 for _ in range(2))
    s_own = scores(ka_ref[pl.ds(pl.multiple_of(i * blk, blk), blk), :], q_plain)
    state = absorb(init, [jnp.where(key_le_query, s, NEG) for s in s_own], vt_own_ref[...])

    last_tile = vt_ref.shape[0] - 1

    def block_scores(block):
        return scores(ka_ref[pl.ds(pl.multiple_of(block * blk, blk), blk), :], qs)

    def past_tile(t, carry):
        state, s_next = carry
        for g in range(blocks_per_tile):
            s_cur = s_next
            if g + 1 < blocks_per_tile:
                s_next = block_scores(t * blocks_per_tile + g + 1)
            else:
                s_next = block_scores(jnp.minimum(t + 1, last_tile) * blocks_per_tile)
            state = absorb(state, s_cur, vt_ref[t, :, g * blk:(g + 1) * blk])
        return state, tuple(s_next)

    n_past_tiles = (i + blocks_per_tile - 1) // blocks_per_tile
    state, _ = lax.fori_loop(0, n_past_tiles, past_tile, (state, tuple(block_scores(0))))
    out_t = jnp.concatenate([acc / l for _, l, acc in state], axis=0)
    o_ref[...] = out_t.T.astype(BF16)


def _moba_attn(qaug, kaug, vtb):
    rows = qaug.shape[0]
    n_tiles = rows // KV_TILE
    blocks_per_tile = KV_TILE // MOBA_BLOCK
    return pl.pallas_call(
        _moba_attn_kernel,
        out_shape=jax.ShapeDtypeStruct((rows, ATTN_WIDTH), BF16),
        grid=(N_PAIRS, rows // MOBA_BLOCK),
        in_specs=[
            pl.BlockSpec((MOBA_BLOCK, 2 * LANES), lambda c, i: (i, c)),
            pl.BlockSpec((rows, 2 * LANES), lambda c, i: (0, c)),
            pl.BlockSpec((n_tiles, LANES, KV_TILE), lambda c, i: (0, c, 0)),
            pl.BlockSpec((None, LANES, MOBA_BLOCK), lambda c, i: (i // blocks_per_tile, c, i % blocks_per_tile)),
        ],
        out_specs=pl.BlockSpec((MOBA_BLOCK, LANES), lambda c, i: (i, c)),
        compiler_params=pltpu.CompilerParams(
            dimension_semantics=("arbitrary", "arbitrary"), vmem_limit_bytes=VMEM_LIMIT_BYTES),
        name="moba_prompt_attn",
    )(qaug, kaug, vtb, vtb)


def _gate_matrices(kmean):
    n_blocks = kmean.shape[0]
    km = kmean.reshape(n_blocks, N_HEADS, HEAD_DIM).transpose(1, 2, 0)
    km = jnp.pad(km, ((0, 0), (0, 0), (0, HALF - n_blocks)))
    zeros = jnp.zeros_like(km)
    even = jnp.concatenate([jnp.concatenate([zeros, km], axis=2), jnp.zeros((N_HEADS, HALF, LANES), F32)], axis=1)
    odd = jnp.concatenate([jnp.zeros((N_HEADS, HALF, LANES), F32), jnp.concatenate([km, zeros], axis=2)], axis=1)
    is_odd = (jnp.arange(N_HEADS) % 2 == 1)[:, None, None]
    return jnp.where(is_odd, odd, even)


def _page_index(b, s, pt_ref, *, layer, slot):
    return (layer, pt_ref[b, s * PAGES_PER_STEP + slot % PAGES_PER_STEP], 0, 0, 0)


def _head_diagonal(x, rows_per_head):
    return jnp.concatenate(
        [x[h * rows_per_head:(h + 1) * rows_per_head, (h // 2) * LANES:(h // 2 + 1) * LANES]
         for h in range(N_HEADS)], axis=0)


def _sample_attn_kernel(pt_ref, qbd_ref, knew_ref, vnew_ref, ownbias_ref, *rest):
    del pt_ref
    k_refs = rest[:PAGES_PER_STEP]
    v_refs = rest[PAGES_PER_STEP:2 * PAGES_PER_STEP]
    o_ref, m_ref, l_ref, g_ref, part_ref = rest[2 * PAGES_PER_STEP:]
    s = pl.program_id(1)
    n_pages = part_ref.shape[0]
    rows = qbd_ref.shape[1]
    tokens = rows // N_HEADS
    lane = lax.broadcasted_iota(jnp.int32, (rows, LANES), 1)
    lane_f = lane.astype(F32)
    qb = (qbd_ref[0] * HEAD_DIM ** -0.5).astype(BF16)

    @pl.when(s == 0)
    def _():
        m_ref[...] = jnp.full(m_ref.shape, NEG, F32)
        l_ref[...] = jnp.zeros(l_ref.shape, F32)
        g_ref[...] = jnp.zeros(g_ref.shape, F32)

    first_page = s * PAGES_PER_STEP
    scores = [_dot(qb, k_refs[slot][...].reshape(ATTN_WIDTH, PAGE_SIZE).astype(BF16))
              for slot in range(PAGES_PER_STEP)]
    m_all, l_all, g_all = m_ref[...], l_ref[...], g_ref[...]
    probs = []
    for slot, sc in enumerate(scores):
        m = jnp.max(sc, axis=-1, keepdims=True)
        p = jnp.exp(sc - m)
        here = lane == first_page + slot
        m_all = jnp.where(here, m, m_all)
        l_all = jnp.where(here, jnp.sum(p, axis=-1, keepdims=True), l_all)
        g_all = jnp.where(here, jnp.sum(sc, axis=-1, keepdims=True), g_all)
        probs.append(p.astype(BF16))
    m_ref[...], l_ref[...], g_ref[...] = m_all, l_all, g_all
    pair_rows = 2 * tokens
    for slot, pb in enumerate(probs):
        vt = v_refs[slot][...].reshape(ATTN_WIDTH, PAGE_SIZE).astype(BF16)
        part_ref[first_page + slot] = jnp.concatenate(
            [_dot_nt(pb[c * pair_rows:(c + 1) * pair_rows], vt[c * LANES:(c + 1) * LANES])
             for c in range(N_PAIRS)], axis=0)

    @pl.when(s == pl.num_programs(1) - 1)
    def _():
        gate = g_ref[...]
        even = jnp.bitwise_and(lane, 1) == 0
        gate = gate + jnp.where(even, pltpu.roll(gate, n_pages - 1, 1), pltpu.roll(gate, 1, 1))
        block_first_lane = (lane - jnp.bitwise_and(lane, 1)).astype(F32)
        keep = jnp.zeros(gate.shape, jnp.bool_)
        for _ in range(MOBA_TOPK):
            gm = jnp.max(gate, axis=-1, keepdims=True)
            first = jnp.min(jnp.where(gate == gm, lane_f, 2.0 * LANES), axis=-1, keepdims=True)
            pick = block_first_lane == first
            keep = keep | pick
            gate = jnp.where(pick, NEG, gate)
        m_all = m_ref[...]
        s_own = _dot_nt(qb, knew_ref[0].astype(BF16)) + ownbias_ref[...]
        m_fin = jnp.maximum(jnp.max(jnp.where(keep, m_all, NEG), axis=-1, keepdims=True),
                            jnp.max(s_own, axis=-1, keepdims=True))
        p_own = jnp.exp(s_own - m_fin)
        w = jnp.where(keep, jnp.exp(m_all - m_fin), 0.0)
        l_fin = jnp.sum(p_own, axis=-1, keepdims=True) + jnp.sum(w * l_ref[...], axis=-1, keepdims=True)
        acc = _head_diagonal(_dot(p_own.astype(BF16), vnew_ref[0].astype(BF16)), tokens)
        for page in range(n_pages):
            acc = acc + w[:, page:page + 1] * part_ref[page]
        acc = acc / l_fin
        row = lax.broadcasted_iota(jnp.int32, (rows, LANES), 0)
        odd_head = jnp.bitwise_and(row, tokens) == tokens
        o_ref[0] = jnp.where(odd_head, pltpu.roll(acc, HALF, 1), acc)[:, :HEAD_DIM]


def _sample_attn(q, k_new, v_new, cache_kt, cache_vt, page_table, layer):
    n_seq, tokens, _ = q.shape
    n_pages = page_table.shape[1]
    rows = N_HEADS * tokens
    assert n_pages == LANES and MOBA_BLOCK == 2 * PAGE_SIZE
    assert tokens <= LANES and tokens & (tokens - 1) == 0
    assert n_pages * PAGE_SIZE // MOBA_BLOCK >= MOBA_TOPK
    q4 = q.reshape(n_seq, tokens, N_HEADS, HEAD_DIM).transpose(0, 2, 1, 3)
    qbd = jnp.einsum("bhtd,hg->bhtgd", q4, jnp.eye(N_HEADS, dtype=F32)).reshape(n_seq, rows, ATTN_WIDTH)
    pad = ((0, 0), (0, LANES - tokens), (0, 0))
    k_new = jnp.pad(k_new, pad)
    v_new = jnp.pad(v_new, pad)
    t_q = jnp.arange(rows)[:, None] % tokens
    ownbias = jnp.where(jnp.arange(LANES)[None, :] <= t_q, 0.0, NEG).astype(F32)
    page_block = (None, None, N_HEADS, HEAD_DIM, PAGE_SIZE)
    page_specs = [pl.BlockSpec(page_block, functools.partial(_page_index, layer=layer, slot=slot))
                  for slot in range(2 * PAGES_PER_STEP)]
    grid_spec = pltpu.PrefetchScalarGridSpec(
        num_scalar_prefetch=1,
        grid=(n_seq, n_pages // PAGES_PER_STEP),
        in_specs=[
            pl.BlockSpec((1, rows, ATTN_WIDTH), lambda b, s, pt: (b, 0, 0)),
            pl.BlockSpec((1, LANES, ATTN_WIDTH), lambda b, s, pt: (b, 0, 0)),
            pl.BlockSpec((1, LANES, ATTN_WIDTH), lambda b, s, pt: (b, 0, 0)),
            pl.BlockSpec(ownbias.shape, lambda b, s, pt: (0, 0)),
        ] + page_specs,
        out_specs=pl.BlockSpec((1, rows, HEAD_DIM), lambda b, s, pt: (b, 0, 0)),
        scratch_shapes=[
            pltpu.VMEM((rows, n_pages), F32),
            pltpu.VMEM((rows, n_pages), F32),
            pltpu.VMEM((rows, n_pages), F32),
            pltpu.VMEM((n_pages, rows, LANES), F32),
        ],
    )
    return pl.pallas_call(
        _sample_attn_kernel,
        out_shape=jax.ShapeDtypeStruct((n_seq, rows, HEAD_DIM), F32),
        grid_spec=grid_spec,
        compiler_params=pltpu.CompilerParams(
            dimension_semantics=("arbitrary", "arbitrary"), vmem_limit_bytes=VMEM_LIMIT_BYTES),
        name="moba_sample_attn",
    )(page_table, qbd, k_new, v_new, ownbias,
      *([cache_kt] * PAGES_PER_STEP), *([cache_vt] * PAGES_PER_STEP))


def _post_kernel(x_ref, a_ref, b_ref, gpre_ref, wg_ref, wpa_ref, wpb_ref, wout_ref,
                 gpm_ref, gpf_ref, wup_ref, wdown_ref, gff_ref, o_ref):
    x = x_ref[...]
    d_model = x.shape[1]
    h = _rmsnorm(x, gpre_ref[...]).astype(BF16)
    gates = jax.nn.sigmoid(_dot(h, wg_ref[...]))
    merged = (gates[:, :d_model] * _dot(a_ref[...], wpa_ref[...])
              + gates[:, d_model:] * _dot(b_ref[...], wpb_ref[...]))
    x = x + _rmsnorm(_dot(merged.astype(BF16), wout_ref[...]), gpm_ref[...])
    h2 = _rmsnorm(x, gpf_ref[...]).astype(BF16)
    slab = wup_ref.shape[1] // FFN_SPLIT
    f = jnp.zeros(x.shape, F32)
    for c in range(FFN_SPLIT):
        up = _dot(h2, wup_ref[:, c * slab:(c + 1) * slab])
        f = f + _dot(jnp.square(jnp.maximum(up, 0.0)).astype(BF16), wdown_ref[c * slab:(c + 1) * slab, :])
    o_ref[...] = x + _rmsnorm(f, gff_ref[...])


def _post(x, a, b, g_pre, w_g, w_pa, w_pb, w_out, g_pm, g_pf, w_up, w_down, g_ff, *, tm):
    rows, d_model = x.shape
    consts = (g_pre, w_g, w_pa, w_pb, w_out, g_pm, g_pf, w_up, w_down, g_ff)
    return pl.pallas_call(
        _post_kernel,
        out_shape=jax.ShapeDtypeStruct((rows, d_model), F32),
        grid=(rows // tm,),
        in_specs=[_row_spec(tm, d_model), _row_spec(tm, SGU_WIDTH), _row_spec(tm, ATTN_WIDTH)]
        + [_const_spec(c.shape) for c in consts],
        out_specs=_row_spec(tm, d_model),
        compiler_params=pltpu.CompilerParams(
            dimension_semantics=("arbitrary",), vmem_limit_bytes=VMEM_LIMIT_BYTES),
        name="merge_mlp",
    )(x, a, b, *consts)


def _pair_weights(w):
    g, p, j = w.shape
    return w.reshape(g // 2, 2, p, j).transpose(0, 2, 1, 3).reshape(g // 2, p, 2 * j)


def kernel(x_prompt, x_sample, cache_k, cache_v, page_table, g_pre_mix, w_in, g_sgu, w_s, b_s, w_pa, w_pb,
           w_out, g_post_mix, g_pre_ffn, w_up, w_down, g_post_ffn):
    depth = w_in.shape[0]
    batch, seq, d_model = x_prompt.shape
    dec_batch, dec_seq, _ = x_sample.shape
    assert batch == 1 and seq % KV_TILE == 0 and KV_TILE % PROMPT_ROWS == 0 and seq // MOBA_BLOCK <= HALF
    assert dec_seq <= CHUNK and SGU_GROUP_DIM == HALF and HEAD_DIM == HALF
    n_qkvuz = 3 * ATTN_WIDTH + 2 * SGU_WIDTH
    dec_rows = dec_batch * dec_seq

    cache_kt = cache_k.transpose(0, 1, 3, 4, 2)
    cache_vt = cache_v.transpose(0, 1, 3, 4, 2)
    xp = x_prompt.reshape(seq, d_model)
    xs = x_sample.reshape(dec_rows, d_model)
    kp_rows, vp_rows, ks_rows, vs_rows, zs_rows = [], [], [], [], []
    for l in range(depth):
        row = lambda g: g[l][None, :]
        w_qkvuz = w_in[l][:, :n_qkvuz].astype(BF16)
        w_gates = w_in[l][:, n_qkvuz:].astype(BF16)
        post_w = (row(g_pre_mix), w_gates, w_pa[l].astype(BF16), w_pb[l].astype(BF16), w_out[l].astype(BF16),
                  row(g_post_mix), row(g_pre_ffn), w_up[l].astype(BF16), w_down[l].astype(BF16), row(g_post_ffn))

        bias_p = jnp.repeat(b_s[l].T, SGU_GROUP_DIM, axis=1)
        q, a, kmean, kaug, kt, vt, vtb = _inproj(
            xp, row(g_pre_mix), w_qkvuz, row(g_sgu), _pair_weights(w_s[l]), bias_p,
            tm=PROMPT_ROWS, chunk=CHUNK, emit_attn=True)
        qaug = _select(q, _gate_matrices(kmean.reshape(seq // MOBA_BLOCK, ATTN_WIDTH)))
        b_attn = _moba_attn(qaug, kaug, vtb)
        xp = _post(xp, a, b_attn, *post_w, tm=PROMPT_ROWS)
        kp_rows.append(kt.reshape(N_HEADS, HEAD_DIM, seq).transpose(2, 0, 1)[None])
        vp_rows.append(vt.reshape(N_HEADS, HEAD_DIM, seq).transpose(2, 0, 1)[None])

        w_blockdiag = jnp.einsum("bc,gpj->gbpcj", jnp.eye(dec_batch, dtype=F32),
                                 w_s[l][:, :dec_seq, :dec_seq]).reshape(SGU_GROUPS, dec_rows, dec_rows)
        bias_s = jnp.tile(bias_p[:dec_seq], (dec_batch, 1))
        q, a, k, v, z = _inproj(
            xs, row(g_pre_mix), w_qkvuz, row(g_sgu), _pair_weights(w_blockdiag), bias_s,
            tm=dec_rows, chunk=dec_rows, emit_attn=False)
        o = _sample_attn(q.reshape(dec_batch, dec_seq, ATTN_WIDTH), k.reshape(dec_batch, dec_seq, ATTN_WIDTH),
                         v.reshape(dec_batch, dec_seq, ATTN_WIDTH), cache_kt, cache_vt, page_table, l)
        b_attn = o.reshape(dec_batch, N_HEADS, dec_seq, HEAD_DIM).transpose(0, 2, 1, 3).reshape(
            dec_rows, ATTN_WIDTH).astype(BF16)
        xs = _post(xs, a, b_attn, *post_w, tm=dec_rows)
        ks_rows.append(k.reshape(dec_batch, dec_seq, N_HEADS, HEAD_DIM))
        vs_rows.append(v.reshape(dec_batch, dec_seq, N_HEADS, HEAD_DIM))
        zs_rows.append(z.reshape(dec_batch, dec_seq, SGU_WIDTH))

    return (xp.reshape(batch, seq, d_model), xs.reshape(dec_batch, dec_seq, d_model),
            jnp.stack(kp_rows), jnp.stack(vp_rows), jnp.stack(ks_rows), jnp.stack(vs_rows), jnp.stack(zs_rows))
```

```python
import functools

import jax
import jax.numpy as jnp
from jax import lax
from jax.experimental import pallas as pl
from jax.experimental.pallas import tpu as pltpu

N_HEADS = 8
HEAD_DIM = 64
ATTN_WIDTH = N_HEADS * HEAD_DIM
SGU_GROUPS = 8
SGU_WIDTH = 512
SGU_GROUP_DIM = SGU_WIDTH // SGU_GROUPS
CHUNK = 128
MOBA_BLOCK = 256
MOBA_TOPK = 3
PAGE_SIZE = 128
EPS = 1e-6
NEG = -1e30
REAL_SCORE_FLOOR = -1e29
LOG2_E = 1.4426950408889634

LANES = 128
HALF = LANES // 2
N_PAIRS = N_HEADS // 2
VMEM_LIMIT_BYTES = 56 * 1024 * 1024

PROMPT_ROWS = 512
FFN_SPLIT = 4
SUM_ROWS = 16
KV_TILE = 1024
PAGES_PER_STEP = 16

F32 = jnp.float32
BF16 = jnp.bfloat16


def _rmsnorm(x, g):
    return x * lax.rsqrt(jnp.mean(x * x, axis=-1, keepdims=True) + EPS) * g


def _dot(a, b):
    return jnp.dot(a, b, preferred_element_type=F32)


def _dot_nt(a, b):
    return lax.dot_general(a, b, (((1,), (1,)), ((), ())), preferred_element_type=F32)


def _const_spec(shape):
    zeros = (0,) * len(shape)
    return pl.BlockSpec(shape, lambda *_: zeros, pipeline_mode=pl.Buffered(1))


def _row_spec(rows, cols):
    return pl.BlockSpec((rows, cols), lambda i: (i, 0))


def _inproj_kernel(x_ref, gpre_ref, w_ref, gsgu_ref, ws_ref, bias_ref, *out_refs, chunk, emit_attn):
    if emit_attn:
        q_ref, a_ref, kmean_ref, kaug_ref, kt_ref, vt_ref, vtb_ref = out_refs
    else:
        q_ref, a_ref, k_ref, v_ref, z_ref = out_refs
    tm = x_ref.shape[0]
    h = _rmsnorm(x_ref[...], gpre_ref[...]).astype(BF16)
    y = _dot(h, w_ref[...])
    q = y[:, 0:ATTN_WIDTH]
    k = y[:, ATTN_WIDTH:2 * ATTN_WIDTH]
    v = y[:, 2 * ATTN_WIDTH:3 * ATTN_WIDTH]
    u = y[:, 3 * ATTN_WIDTH:3 * ATTN_WIDTH + SGU_WIDTH]
    z = y[:, 3 * ATTN_WIDTH + SGU_WIDTH:]
    q_ref[...] = q
    zn = _rmsnorm(jax.nn.gelu(z), gsgu_ref[...])
    if not emit_attn:
        k_ref[...] = k
        v_ref[...] = v
        z_ref[...] = zn

    n_chunks = tm // chunk
    low = lax.broadcasted_iota(jnp.int32, (chunk, LANES), 1) < HALF
    w_row = lax.broadcasted_iota(jnp.int32, (chunk, 2 * chunk), 0)
    w_col = lax.broadcasted_iota(jnp.int32, (chunk, 2 * chunk), 1)
    causal = jnp.where(w_col >= chunk, w_col - chunk, w_col) <= w_row
    pair_out = []
    for p in range(N_PAIRS):
        w = jnp.where(causal, ws_ref[p], 0.0).astype(BF16)
        cols = []
        for c in range(n_chunks):
            zc = zn[c * chunk:(c + 1) * chunk, p * LANES:(p + 1) * LANES]
            cols.append(jnp.concatenate([jnp.where(low, zc, 0.0), jnp.where(low, 0.0, zc)], axis=0))
        rhs = jnp.concatenate(cols, axis=1).astype(BF16)
        pair_out.append(_dot(w, rhs))
    bias = bias_ref[...]
    rows = []
    for c in range(n_chunks):
        rows.append(jnp.concatenate([po[:, c * LANES:(c + 1) * LANES] for po in pair_out], axis=1) + bias)
    s = jnp.concatenate(rows, axis=0)
    a_ref[...] = (jax.nn.gelu(u) * s).astype(BF16)

    if emit_attn:
        kmean_ref[0] = jnp.mean(k.reshape(tm // MOBA_BLOCK, MOBA_BLOCK, ATTN_WIDTH), axis=1)
        row_g = pl.program_id(0) * tm + lax.broadcasted_iota(jnp.int32, (tm, LANES), 0)
        blk = jnp.right_shift(row_g, MOBA_BLOCK.bit_length() - 1)
        lane = lax.broadcasted_iota(jnp.int32, (tm, LANES), 1)
        low_t = lane < HALF
        onehot_hi = jnp.where(lane - HALF == blk, 1.0, 0.0)
        onehot_lo = jnp.where(lane == blk, 1.0, 0.0)
        for p in range(N_PAIRS):
            kc = k[:, p * LANES:(p + 1) * LANES]
            kaug_ref[:, (2 * p) * LANES:(2 * p + 1) * LANES] = jnp.where(low_t, kc, onehot_hi).astype(BF16)
            kaug_ref[:, (2 * p + 1) * LANES:(2 * p + 2) * LANES] = jnp.where(low_t, onehot_lo, kc).astype(BF16)
        kt_ref[...] = k.T
        vt = v.T
        vt_ref[...] = vt
        vtb_ref[0] = vt.astype(BF16)


def _inproj(x, g_pre, w_qkvuz, g_sgu, ws_pairs, bias, *, tm, chunk, emit_attn):
    rows, d_model = x.shape
    n_tiles = rows // tm
    out_shape = [jax.ShapeDtypeStruct((rows, ATTN_WIDTH), F32), jax.ShapeDtypeStruct((rows, SGU_WIDTH), BF16)]
    out_specs = [_row_spec(tm, ATTN_WIDTH), _row_spec(tm, SGU_WIDTH)]
    if emit_attn:
        blocks_per_tile = tm // MOBA_BLOCK
        out_shape += [
            jax.ShapeDtypeStruct((n_tiles, blocks_per_tile, ATTN_WIDTH), F32),
            jax.ShapeDtypeStruct((rows, N_HEADS * LANES), BF16),
            jax.ShapeDtypeStruct((ATTN_WIDTH, rows), F32),
            jax.ShapeDtypeStruct((ATTN_WIDTH, rows), F32),
            jax.ShapeDtypeStruct((rows // KV_TILE, ATTN_WIDTH, KV_TILE), BF16),
        ]
        tiles_per_kv = KV_TILE // tm
        out_specs += [
            pl.BlockSpec((1, blocks_per_tile, ATTN_WIDTH), lambda i: (i, 0, 0)),
            _row_spec(tm, N_HEADS * LANES),
            pl.BlockSpec((ATTN_WIDTH, tm), lambda i: (0, i)),
            pl.BlockSpec((ATTN_WIDTH, tm), lambda i: (0, i)),
            pl.BlockSpec((1, ATTN_WIDTH, tm), lambda i: (i // tiles_per_kv, 0, i % tiles_per_kv)),
        ]
    else:
        out_shape += [jax.ShapeDtypeStruct((rows, ATTN_WIDTH), F32)] * 2 + [
            jax.ShapeDtypeStruct((rows, SGU_WIDTH), F32)]
        out_specs += [_row_spec(tm, ATTN_WIDTH)] * 2 + [_row_spec(tm, SGU_WIDTH)]
    return pl.pallas_call(
        functools.partial(_inproj_kernel, chunk=chunk, emit_attn=emit_attn),
        out_shape=out_shape,
        grid=(n_tiles,),
        in_specs=[
            _row_spec(tm, d_model),
            _const_spec(g_pre.shape),
            _const_spec(w_qkvuz.shape),
            _const_spec(g_sgu.shape),
            _const_spec(ws_pairs.shape),
            _const_spec(bias.shape),
        ],
        out_specs=out_specs,
        compiler_params=pltpu.CompilerParams(
            dimension_semantics=("arbitrary",), vmem_limit_bytes=VMEM_LIMIT_BYTES),
        name="inproj_attn" if emit_attn else "inproj",
    )(x, g_pre, w_qkvuz, g_sgu, ws_pairs, bias)


def _select_kernel(q_ref, gmat_ref, qaug_ref):
    i_f = pl.program_id(0).astype(F32)
    rows = q_ref.shape[0]
    lane = lax.broadcasted_iota(jnp.int32, (rows, LANES), 1)
    low = lane < HALF
    scale = HEAD_DIM ** -0.5 * LOG2_E
    for p in range(N_PAIRS):
        qc = q_ref[:, p * LANES:(p + 1) * LANES]
        for parity in range(2):
            h = 2 * p + parity
            gate_lanes = low if parity else jnp.logical_not(low)
            block_of_lane = (lane if parity else lane - HALF).astype(F32)
            g = jnp.dot(qc, gmat_ref[h], precision=lax.Precision.HIGHEST, preferred_element_type=F32)
            g = jnp.where(gate_lanes & (block_of_lane < i_f), g, NEG)
            keep = jnp.zeros(g.shape, jnp.bool_)
            for _ in range(MOBA_TOPK):
                m = jnp.max(g, axis=-1, keepdims=True)
                first = jnp.min(jnp.where(g == m, block_of_lane, 2.0 * LANES), axis=-1, keepdims=True)
                pick = gate_lanes & (block_of_lane == first)
                keep = keep | (pick & (g > REAL_SCORE_FLOOR))
                g = jnp.where(pick, NEG, g)
            mask_row = jnp.where(keep, 0.0, NEG)
            qaug_ref[:, h * LANES:(h + 1) * LANES] = jnp.where(gate_lanes, mask_row, qc * scale).astype(BF16)


def _select(q, gmat):
    rows = q.shape[0]
    return pl.pallas_call(
        _select_kernel,
        out_shape=jax.ShapeDtypeStruct((rows, N_HEADS * LANES), BF16),
        grid=(rows // MOBA_BLOCK,),
        in_specs=[_row_spec(MOBA_BLOCK, ATTN_WIDTH), _const_spec(gmat.shape)],
        out_specs=_row_spec(MOBA_BLOCK, N_HEADS * LANES),
        compiler_params=pltpu.CompilerParams(
            dimension_semantics=("arbitrary",), vmem_limit_bytes=VMEM_LIMIT_BYTES),
        name="moba_select",
    )(q, gmat)


def _moba_attn_kernel(qa_ref, ka_ref, vt_ref, vt_own_ref, o_ref):
    i = pl.program_id(1)
    blk = qa_ref.shape[0]
    tile = vt_ref.shape[2]
    blocks_per_tile = tile // blk
    lane = lax.broadcasted_iota(jnp.int32, (blk, LANES), 1)
    qs = [qa_ref[:, p * LANES:(p + 1) * LANES] for p in range(2)]

    def scores(k_rows, qs):
        return [_dot_nt(k_rows[:, p * LANES:(p + 1) * LANES], qs[p]) for p in range(2)]

    ones_rows = jnp.ones((SUM_ROWS, blk), BF16)

    def absorb(state, s_pair, vt_cols):
        new = []
        for p in range(2):
            m, acc = state[p]
            s = s_pair[p]
            m_new = jnp.maximum(m, jnp.max(s, axis=0, keepdims=True))
            alpha = jnp.exp2(m - m_new)
            e = jnp.exp2((s - m_new).astype(BF16))
            vt_aug = jnp.concatenate([vt_cols[p * HALF:(p + 1) * HALF], ones_rows], axis=0)
            acc = alpha * acc + _dot(vt_aug, e)
            new.append((m_new, acc))
        return tuple(new)

    q_plain = [jnp.where(lane < HALF if p == 0 else lane >= HALF, qs[p], jnp.zeros_like(qs[p]))
               for p in range(2)]
    key_le_query = (lax.broadcasted_iota(jnp.int32, (blk, blk), 0)
                    <= lax.broadcasted_iota(jnp.int32, (blk, blk), 1))
    init = tuple((jnp.full((1, blk), NEG, F32), jnp.zeros((HALF + SUM_ROWS, blk), F32)) for _ in range(2))
    s_own = scores(ka_ref[pl.ds(pl.multiple_of(i * blk, blk), blk), :], q_plain)
    state = absorb(init, [jnp.where(key_le_query, s, NEG) for s in s_own], vt_own_ref[...])

    last_tile = vt_ref.shape[0] - 1

    def block_scores(block):
        return scores(ka_ref[pl.ds(pl.multiple_of(block * blk, blk), blk), :], qs)

    def past_tile(t, carry):
        state, s_next = carry
        for g in range(blocks_per_tile):
            s_cur = s_next
            if g + 1 < blocks_per_tile:
                s_next = block_scores(t * blocks_per_tile + g + 1)
            else:
                s_next = block_scores(jnp.minimum(t + 1, last_tile) * blocks_per_tile)
            state = absorb(state, s_cur, vt_ref[t, :, g * blk:(g + 1) * blk])
        return state, tuple(s_next)

    n_past_tiles = (i + blocks_per_tile - 1) // blocks_per_tile
    state, _ = lax.fori_loop(0, n_past_tiles, past_tile, (state, tuple(block_scores(0))))
    out_t = jnp.concatenate([acc[:HALF] / acc[HALF:HALF + 1] for _, acc in state], axis=0)
    o_ref[...] = out_t.T.astype(BF16)


def _moba_attn(qaug, kaug, vtb):
    rows = qaug.shape[0]
    n_tiles = rows // KV_TILE
    blocks_per_tile = KV_TILE // MOBA_BLOCK
    return pl.pallas_call(
        _moba_attn_kernel,
        out_shape=jax.ShapeDtypeStruct((rows, ATTN_WIDTH), BF16),
        grid=(N_PAIRS, rows // MOBA_BLOCK),
        in_specs=[
            pl.BlockSpec((MOBA_BLOCK, 2 * LANES), lambda c, i: (i, c)),
            pl.BlockSpec((rows, 2 * LANES), lambda c, i: (0, c)),
            pl.BlockSpec((n_tiles, LANES, KV_TILE), lambda c, i: (0, c, 0)),
            pl.BlockSpec((None, LANES, MOBA_BLOCK), lambda c, i: (i // blocks_per_tile, c, i % blocks_per_tile)),
        ],
        out_specs=pl.BlockSpec((MOBA_BLOCK, LANES), lambda c, i: (i, c)),
        compiler_params=pltpu.CompilerParams(
            dimension_semantics=("arbitrary", "arbitrary"), vmem_limit_bytes=VMEM_LIMIT_BYTES),
        name="moba_prompt_attn",
    )(qaug, kaug, vtb, vtb)


def _gate_matrices(kmean):
    n_blocks = kmean.shape[0]
    km = kmean.reshape(n_blocks, N_HEADS, HEAD_DIM).transpose(1, 2, 0)
    km = jnp.pad(km, ((0, 0), (0, 0), (0, HALF - n_blocks)))
    zeros = jnp.zeros_like(km)
    even = jnp.concatenate([jnp.concatenate([zeros, km], axis=2), jnp.zeros((N_HEADS, HALF, LANES), F32)], axis=1)
    odd = jnp.concatenate([jnp.zeros((N_HEADS, HALF, LANES), F32), jnp.concatenate([km, zeros], axis=2)], axis=1)
    is_odd = (jnp.arange(N_HEADS) % 2 == 1)[:, None, None]
    return jnp.where(is_odd, odd, even)


def _page_index(b, s, pt_ref, *, layer, slot):
    return (layer, pt_ref[b, s * PAGES_PER_STEP + slot % PAGES_PER_STEP], 0, 0, 0)


def _head_diagonal(x, rows_per_head):
    return jnp.concatenate(
        [x[h * rows_per_head:(h + 1) * rows_per_head, (h // 2) * LANES:(h // 2 + 1) * LANES]
         for h in range(N_HEADS)], axis=0)


def _sample_attn_kernel(pt_ref, qbd_ref, knew_ref, vnew_ref, ownbias_ref, *rest):
    del pt_ref
    k_refs = rest[:PAGES_PER_STEP]
    v_refs = rest[PAGES_PER_STEP:2 * PAGES_PER_STEP]
    o_ref, m_ref, l_ref, g_ref, part_ref = rest[2 * PAGES_PER_STEP:]
    s = pl.program_id(1)
    n_pages = part_ref.shape[0]
    rows = qbd_ref.shape[1]
    tokens = rows // N_HEADS
    lane = lax.broadcasted_iota(jnp.int32, (rows, LANES), 1)
    lane_f = lane.astype(F32)
    qb = (qbd_ref[0] * HEAD_DIM ** -0.5).astype(BF16)

    @pl.when(s == 0)
    def _():
        m_ref[...] = jnp.full(m_ref.shape, NEG, F32)
        l_ref[...] = jnp.zeros(l_ref.shape, F32)
        g_ref[...] = jnp.zeros(g_ref.shape, F32)

    first_page = s * PAGES_PER_STEP
    scores = [_dot(qb, k_refs[slot][...].reshape(ATTN_WIDTH, PAGE_SIZE).astype(BF16))
              for slot in range(PAGES_PER_STEP)]
    m_all, l_all, g_all = m_ref[...], l_ref[...], g_ref[...]
    probs = []
    for slot, sc in enumerate(scores):
        m = jnp.max(sc, axis=-1, keepdims=True)
        p = jnp.exp(sc - m)
        here = lane == first_page + slot
        m_all = jnp.where(here, m, m_all)
        l_all = jnp.where(here, jnp.sum(p, axis=-1, keepdims=True), l_all)
        g_all = jnp.where(here, jnp.sum(sc, axis=-1, keepdims=True), g_all)
        probs.append(p.astype(BF16))
    m_ref[...], l_ref[...], g_ref[...] = m_all, l_all, g_all
    pair_rows = 2 * tokens
    for slot, pb in enumerate(probs):
        vt = v_refs[slot][...].reshape(ATTN_WIDTH, PAGE_SIZE).astype(BF16)
        part_ref[first_page + slot] = jnp.concatenate(
            [_dot_nt(pb[c * pair_rows:(c + 1) * pair_rows], vt[c * LANES:(c + 1) * LANES])
             for c in range(N_PAIRS)], axis=0)

    @pl.when(s == pl.num_programs(1) - 1)
    def _():
        gate = g_ref[...]
        even = jnp.bitwise_and(lane, 1) == 0
        gate = gate + jnp.where(even, pltpu.roll(gate, n_pages - 1, 1), pltpu.roll(gate, 1, 1))
        block_first_lane = (lane - jnp.bitwise_and(lane, 1)).astype(F32)
        keep = jnp.zeros(gate.shape, jnp.bool_)
        for _ in range(MOBA_TOPK):
            gm = jnp.max(gate, axis=-1, keepdims=True)
            first = jnp.min(jnp.where(gate == gm, lane_f, 2.0 * LANES), axis=-1, keepdims=True)
            pick = block_first_lane == first
            keep = keep | pick
            gate = jnp.where(pick, NEG, gate)
        m_all = m_ref[...]
        s_own = _dot_nt(qb, knew_ref[0].astype(BF16)) + ownbias_ref[...]
        m_fin = jnp.maximum(jnp.max(jnp.where(keep, m_all, NEG), axis=-1, keepdims=True),
                            jnp.max(s_own, axis=-1, keepdims=True))
        p_own = jnp.exp(s_own - m_fin)
        w = jnp.where(keep, jnp.exp(m_all - m_fin), 0.0)
        l_fin = jnp.sum(p_own, axis=-1, keepdims=True) + jnp.sum(w * l_ref[...], axis=-1, keepdims=True)
        acc = _head_diagonal(_dot(p_own.astype(BF16), vnew_ref[0].astype(BF16)), tokens)
        for page in range(n_pages):
            acc = acc + w[:, page:page + 1] * part_ref[page]
        acc = acc / l_fin
        row = lax.broadcasted_iota(jnp.int32, (rows, LANES), 0)
        odd_head = jnp.bitwise_and(row, tokens) == tokens
        o_ref[0] = jnp.where(odd_head, pltpu.roll(acc, HALF, 1), acc)[:, :HEAD_DIM]


def _sample_attn(q, k_new, v_new, cache_kt, cache_vt, page_table, layer):
    n_seq, tokens, _ = q.shape
    n_pages = page_table.shape[1]
    rows = N_HEADS * tokens
    assert n_pages == LANES and MOBA_BLOCK == 2 * PAGE_SIZE
    assert tokens <= LANES and tokens & (tokens - 1) == 0
    assert n_pages * PAGE_SIZE // MOBA_BLOCK >= MOBA_TOPK
    q4 = q.reshape(n_seq, tokens, N_HEADS, HEAD_DIM).transpose(0, 2, 1, 3)
    qbd = jnp.einsum("bhtd,hg->bhtgd", q4, jnp.eye(N_HEADS, dtype=F32)).reshape(n_seq, rows, ATTN_WIDTH)
    pad = ((0, 0), (0, LANES - tokens), (0, 0))
    k_new = jnp.pad(k_new, pad)
    v_new = jnp.pad(v_new, pad)
    t_q = jnp.arange(rows)[:, None] % tokens
    ownbias = jnp.where(jnp.arange(LANES)[None, :] <= t_q, 0.0, NEG).astype(F32)
    page_block = (None, None, N_HEADS, HEAD_DIM, PAGE_SIZE)
    page_specs = [pl.BlockSpec(page_block, functools.partial(_page_index, layer=layer, slot=slot))
                  for slot in range(2 * PAGES_PER_STEP)]
    grid_spec = pltpu.PrefetchScalarGridSpec(
        num_scalar_prefetch=1,
        grid=(n_seq, n_pages // PAGES_PER_STEP),
        in_specs=[
            pl.BlockSpec((1, rows, ATTN_WIDTH), lambda b, s, pt: (b, 0, 0)),
            pl.BlockSpec((1, LANES, ATTN_WIDTH), lambda b, s, pt: (b, 0, 0)),
            pl.BlockSpec((1, LANES, ATTN_WIDTH), lambda b, s, pt: (b, 0, 0)),
            pl.BlockSpec(ownbias.shape, lambda b, s, pt: (0, 0)),
        ] + page_specs,
        out_specs=pl.BlockSpec((1, rows, HEAD_DIM), lambda b, s, pt: (b, 0, 0)),
        scratch_shapes=[
            pltpu.VMEM((rows, n_pages), F32),
            pltpu.VMEM((rows, n_pages), F32),
            pltpu.VMEM((rows, n_pages), F32),
            pltpu.VMEM((n_pages, rows, LANES), F32),
        ],
    )
    return pl.pallas_call(
        _sample_attn_kernel,
        out_shape=jax.ShapeDtypeStruct((n_seq, rows, HEAD_DIM), F32),
        grid_spec=grid_spec,
        compiler_params=pltpu.CompilerParams(
            dimension_semantics=("arbitrary", "arbitrary"), vmem_limit_bytes=VMEM_LIMIT_BYTES),
        name="moba_sample_attn",
    )(page_table, qbd, k_new, v_new, ownbias,
      *([cache_kt] * PAGES_PER_STEP), *([cache_vt] * PAGES_PER_STEP))


def _post_kernel(x_ref, a_ref, b_ref, gpre_ref, wg_ref, wpa_ref, wpb_ref, wout_ref,
                 gpm_ref, gpf_ref, wup_ref, wdown_ref, gff_ref, o_ref):
    x = x_ref[...]
    d_model = x.shape[1]
    h = _rmsnorm(x, gpre_ref[...]).astype(BF16)
    gates = jax.nn.sigmoid(_dot(h, wg_ref[...]))
    merged = (gates[:, :d_model] * _dot(a_ref[...], wpa_ref[...])
              + gates[:, d_model:] * _dot(b_ref[...], wpb_ref[...]))
    x = x + _rmsnorm(_dot(merged.astype(BF16), wout_ref[...]), gpm_ref[...])
    h2 = _rmsnorm(x, gpf_ref[...]).astype(BF16)
    slab = wup_ref.shape[1] // FFN_SPLIT
    f = jnp.zeros(x.shape, F32)
    for c in range(FFN_SPLIT):
        up = _dot(h2, wup_ref[:, c * slab:(c + 1) * slab])
        f = f + _dot(jnp.square(jnp.maximum(up, 0.0)).astype(BF16), wdown_ref[c * slab:(c + 1) * slab, :])
    o_ref[...] = x + _rmsnorm(f, gff_ref[...])


def _post(x, a, b, g_pre, w_g, w_pa, w_pb, w_out, g_pm, g_pf, w_up, w_down, g_ff, *, tm):
    rows, d_model = x.shape
    consts = (g_pre, w_g, w_pa, w_pb, w_out, g_pm, g_pf, w_up, w_down, g_ff)
    return pl.pallas_call(
        _post_kernel,
        out_shape=jax.ShapeDtypeStruct((rows, d_model), F32),
        grid=(rows // tm,),
        in_specs=[_row_spec(tm, d_model), _row_spec(tm, SGU_WIDTH), _row_spec(tm, ATTN_WIDTH)]
        + [_const_spec(c.shape) for c in consts],
        out_specs=_row_spec(tm, d_model),
        compiler_params=pltpu.CompilerParams(
            dimension_semantics=("arbitrary",), vmem_limit_bytes=VMEM_LIMIT_BYTES),
        name="merge_mlp",
    )(x, a, b, *consts)


def _pair_weights(w):
    g, p, j = w.shape
    return w.reshape(g // 2, 2, p, j).transpose(0, 2, 1, 3).reshape(g // 2, p, 2 * j)


def kernel(x_prompt, x_sample, cache_k, cache_v, page_table, g_pre_mix, w_in, g_sgu, w_s, b_s, w_pa, w_pb,
           w_out, g_post_mix, g_pre_ffn, w_up, w_down, g_post_ffn):
    depth = w_in.shape[0]
    batch, seq, d_model = x_prompt.shape
    dec_batch, dec_seq, _ = x_sample.shape
    assert batch == 1 and seq % KV_TILE == 0 and KV_TILE % PROMPT_ROWS == 0 and seq // MOBA_BLOCK <= HALF
    assert dec_seq <= CHUNK and SGU_GROUP_DIM == HALF and HEAD_DIM == HALF
    n_qkvuz = 3 * ATTN_WIDTH + 2 * SGU_WIDTH
    dec_rows = dec_batch * dec_seq

    cache_kt = cache_k.transpose(0, 1, 3, 4, 2)
    cache_vt = cache_v.transpose(0, 1, 3, 4, 2)
    xp = x_prompt.reshape(seq, d_model)
    xs = x_sample.reshape(dec_rows, d_model)
    kp_rows, vp_rows, ks_rows, vs_rows, zs_rows = [], [], [], [], []
    for l in range(depth):
        row = lambda g: g[l][None, :]
        w_qkvuz = w_in[l][:, :n_qkvuz].astype(BF16)
        w_gates = w_in[l][:, n_qkvuz:].astype(BF16)
        post_w = (row(g_pre_mix), w_gates, w_pa[l].astype(BF16), w_pb[l].astype(BF16), w_out[l].astype(BF16),
                  row(g_post_mix), row(g_pre_ffn), w_up[l].astype(BF16), w_down[l].astype(BF16), row(g_post_ffn))

        bias_p = jnp.repeat(b_s[l].T, SGU_GROUP_DIM, axis=1)
        q, a, kmean, kaug, kt, vt, vtb = _inproj(
            xp, row(g_pre_mix), w_qkvuz, row(g_sgu), _pair_weights(w_s[l]), bias_p,
            tm=PROMPT_ROWS, chunk=CHUNK, emit_attn=True)
        qaug = _select(q, _gate_matrices(kmean.reshape(seq // MOBA_BLOCK, ATTN_WIDTH)))
        b_attn = _moba_attn(qaug, kaug, vtb)
        xp = _post(xp, a, b_attn, *post_w, tm=PROMPT_ROWS)
        kp_rows.append(kt.reshape(N_HEADS, HEAD_DIM, seq).transpose(2, 0, 1)[None])
        vp_rows.append(vt.reshape(N_HEADS, HEAD_DIM, seq).transpose(2, 0, 1)[None])

        w_blockdiag = jnp.einsum("bc,gpj->gbpcj", jnp.eye(dec_batch, dtype=F32),
                                 w_s[l][:, :dec_seq, :dec_seq]).reshape(SGU_GROUPS, dec_rows, dec_rows)
        bias_s = jnp.tile(bias_p[:dec_seq], (dec_batch, 1))
        q, a, k, v, z = _inproj(
            xs, row(g_pre_mix), w_qkvuz, row(g_sgu), _pair_weights(w_blockdiag), bias_s,
            tm=dec_rows, chunk=dec_rows, emit_attn=False)
        o = _sample_attn(q.reshape(dec_batch, dec_seq, ATTN_WIDTH), k.reshape(dec_batch, dec_seq, ATTN_WIDTH),
                         v.reshape(dec_batch, dec_seq, ATTN_WIDTH), cache_kt, cache_vt, page_table, l)
        b_attn = o.reshape(dec_batch, N_HEADS, dec_seq, HEAD_DIM).transpose(0, 2, 1, 3).reshape(
            dec_rows, ATTN_WIDTH).astype(BF16)
        xs = _post(xs, a, b_attn, *post_w, tm=dec_rows)
        ks_rows.append(k.reshape(dec_batch, dec_seq, N_HEADS, HEAD_DIM))
        vs_rows.append(v.reshape(dec_batch, dec_seq, N_HEADS, HEAD_DIM))
        zs_rows.append(z.reshape(dec_batch, dec_seq, SGU_WIDTH))

    return (xp.reshape(batch, seq, d_model), xs.reshape(dec_batch, dec_seq, d_model),
            jnp.stack(kp_rows), jnp.stack(vp_rows), jnp.stack(ks_rows), jnp.stack(vs_rows), jnp.stack(zs_rows))
```

```python
import functools

import jax
import jax.numpy as jnp
from jax import lax
from jax.experimental import pallas as pl
from jax.experimental.pallas import tpu as pltpu

N_HEADS = 8
HEAD_DIM = 64
ATTN_WIDTH = N_HEADS * HEAD_DIM
SGU_GROUPS = 8
SGU_WIDTH = 512
SGU_GROUP_DIM = SGU_WIDTH // SGU_GROUPS
CHUNK = 128
MOBA_BLOCK = 256
MOBA_TOPK = 3
PAGE_SIZE = 128
EPS = 1e-6
NEG = -1e30
REAL_SCORE_FLOOR = -1e29
LOG2_E = 1.4426950408889634

LANES = 128
HALF = LANES // 2
N_PAIRS = N_HEADS // 2
VMEM_LIMIT_BYTES = 56 * 1024 * 1024

PROMPT_ROWS = 512
FFN_SPLIT = 4
SUM_ROWS = 16
KV_TILE = 1024
PAGES_PER_STEP = 16

F32 = jnp.float32
BF16 = jnp.bfloat16


def _rmsnorm(x, g):
    return x * lax.rsqrt(jnp.mean(x * x, axis=-1, keepdims=True) + EPS) * g


def _dot(a, b):
    return jnp.dot(a, b, preferred_element_type=F32)


def _dot_nt(a, b):
    return lax.dot_general(a, b, (((1,), (1,)), ((), ())), preferred_element_type=F32)


def _const_spec(shape):
    zeros = (0,) * len(shape)
    return pl.BlockSpec(shape, lambda *_: zeros, pipeline_mode=pl.Buffered(1))


def _row_spec(rows, cols):
    return pl.BlockSpec((rows, cols), lambda i: (i, 0))


def _inproj_kernel(x_ref, gpre_ref, w_ref, gsgu_ref, ws_ref, bias_ref, *out_refs, chunk, emit_attn):
    if emit_attn:
        q_ref, a_ref, kmean_ref, kaug_ref, kt_ref, vt_ref, vtb_ref = out_refs
    else:
        q_ref, a_ref, k_ref, v_ref, z_ref = out_refs
    tm = x_ref.shape[0]
    h = _rmsnorm(x_ref[...], gpre_ref[...]).astype(BF16)
    y = _dot(h, w_ref[...])
    q = y[:, 0:ATTN_WIDTH]
    k = y[:, ATTN_WIDTH:2 * ATTN_WIDTH]
    v = y[:, 2 * ATTN_WIDTH:3 * ATTN_WIDTH]
    u = y[:, 3 * ATTN_WIDTH:3 * ATTN_WIDTH + SGU_WIDTH]
    z = y[:, 3 * ATTN_WIDTH + SGU_WIDTH:]
    q_ref[...] = q.T if emit_attn else q
    zn = _rmsnorm(jax.nn.gelu(z), gsgu_ref[...])
    if not emit_attn:
        k_ref[...] = k
        v_ref[...] = v
        z_ref[...] = zn

    n_chunks = tm // chunk
    low = lax.broadcasted_iota(jnp.int32, (chunk, LANES), 1) < HALF
    w_row = lax.broadcasted_iota(jnp.int32, (chunk, 2 * chunk), 0)
    w_col = lax.broadcasted_iota(jnp.int32, (chunk, 2 * chunk), 1)
    causal = jnp.where(w_col >= chunk, w_col - chunk, w_col) <= w_row
    pair_out = []
    for p in range(N_PAIRS):
        w = jnp.where(causal, ws_ref[p], 0.0).astype(BF16)
        cols = []
        for c in range(n_chunks):
            zc = zn[c * chunk:(c + 1) * chunk, p * LANES:(p + 1) * LANES]
            cols.append(jnp.concatenate([jnp.where(low, zc, 0.0), jnp.where(low, 0.0, zc)], axis=0))
        rhs = jnp.concatenate(cols, axis=1).astype(BF16)
        pair_out.append(_dot(w, rhs))
    bias = bias_ref[...]
    rows = []
    for c in range(n_chunks):
        rows.append(jnp.concatenate([po[:, c * LANES:(c + 1) * LANES] for po in pair_out], axis=1) + bias)
    s = jnp.concatenate(rows, axis=0)
    a_ref[...] = (jax.nn.gelu(u) * s).astype(BF16)

    if emit_attn:
        kmean_ref[0] = jnp.mean(k.reshape(tm // MOBA_BLOCK, MOBA_BLOCK, ATTN_WIDTH), axis=1)
        row_g = pl.program_id(0) * tm + lax.broadcasted_iota(jnp.int32, (tm, LANES), 0)
        blk = jnp.right_shift(row_g, MOBA_BLOCK.bit_length() - 1)
        lane = lax.broadcasted_iota(jnp.int32, (tm, LANES), 1)
        low_t = lane < HALF
        onehot_hi = jnp.where(lane - HALF == blk, 1.0, 0.0)
        onehot_lo = jnp.where(lane == blk, 1.0, 0.0)
        for p in range(N_PAIRS):
            kc = k[:, p * LANES:(p + 1) * LANES]
            kaug_ref[:, (2 * p) * LANES:(2 * p + 1) * LANES] = jnp.where(low_t, kc, onehot_hi).astype(BF16)
            kaug_ref[:, (2 * p + 1) * LANES:(2 * p + 2) * LANES] = jnp.where(low_t, onehot_lo, kc).astype(BF16)
        kt_ref[...] = k.T
        vt = v.T
        vt_ref[...] = vt
        vtb_ref[0] = vt.astype(BF16)


def _inproj(x, g_pre, w_qkvuz, g_sgu, ws_pairs, bias, *, tm, chunk, emit_attn):
    rows, d_model = x.shape
    n_tiles = rows // tm
    out_shape = [jax.ShapeDtypeStruct((rows, ATTN_WIDTH), F32), jax.ShapeDtypeStruct((rows, SGU_WIDTH), BF16)]
    out_specs = [_row_spec(tm, ATTN_WIDTH), _row_spec(tm, SGU_WIDTH)]
    if emit_attn:
        out_shape[0] = jax.ShapeDtypeStruct((ATTN_WIDTH, rows), F32)
        out_specs[0] = pl.BlockSpec((ATTN_WIDTH, tm), lambda i: (0, i))
        blocks_per_tile = tm // MOBA_BLOCK
        out_shape += [
            jax.ShapeDtypeStruct((n_tiles, blocks_per_tile, ATTN_WIDTH), F32),
            jax.ShapeDtypeStruct((rows, N_HEADS * LANES), BF16),
            jax.ShapeDtypeStruct((ATTN_WIDTH, rows), F32),
            jax.ShapeDtypeStruct((ATTN_WIDTH, rows), F32),
            jax.ShapeDtypeStruct((rows // KV_TILE, ATTN_WIDTH, KV_TILE), BF16),
        ]
        tiles_per_kv = KV_TILE // tm
        out_specs += [
            pl.BlockSpec((1, blocks_per_tile, ATTN_WIDTH), lambda i: (i, 0, 0)),
            _row_spec(tm, N_HEADS * LANES),
            pl.BlockSpec((ATTN_WIDTH, tm), lambda i: (0, i)),
            pl.BlockSpec((ATTN_WIDTH, tm), lambda i: (0, i)),
            pl.BlockSpec((1, ATTN_WIDTH, tm), lambda i: (i // tiles_per_kv, 0, i % tiles_per_kv)),
        ]
    else:
        out_shape += [jax.ShapeDtypeStruct((rows, ATTN_WIDTH), F32)] * 2 + [
            jax.ShapeDtypeStruct((rows, SGU_WIDTH), F32)]
        out_specs += [_row_spec(tm, ATTN_WIDTH)] * 2 + [_row_spec(tm, SGU_WIDTH)]
    return pl.pallas_call(
        functools.partial(_inproj_kernel, chunk=chunk, emit_attn=emit_attn),
        out_shape=out_shape,
        grid=(n_tiles,),
        in_specs=[
            _row_spec(tm, d_model),
            _const_spec(g_pre.shape),
            _const_spec(w_qkvuz.shape),
            _const_spec(g_sgu.shape),
            _const_spec(ws_pairs.shape),
            _const_spec(bias.shape),
        ],
        out_specs=out_specs,
        compiler_params=pltpu.CompilerParams(
            dimension_semantics=("arbitrary",), vmem_limit_bytes=VMEM_LIMIT_BYTES),
        name="inproj_attn" if emit_attn else "inproj",
    )(x, g_pre, w_qkvuz, g_sgu, ws_pairs, bias)


def _select_kernel(qt_ref, ghi_ref, glo_ref, qaug_ref):
    i_f = pl.program_id(0).astype(F32)
    n_q = qt_ref.shape[1]
    scale = HEAD_DIM ** -0.5 * LOG2_E
    qt = qt_ref[...]
    q_hi = qt.astype(BF16)
    q_lo = (qt - q_hi.astype(F32)).astype(BF16)
    gates = _dot(ghi_ref[...], q_hi) + (_dot(ghi_ref[...], q_lo) + _dot(glo_ref[...], q_hi))
    block = lax.broadcasted_iota(jnp.int32, (HALF, n_q), 0).astype(F32)
    no_block = 2.0 * LANES
    for h in range(N_HEADS):
        rows = slice(h * HALF, (h + 1) * HALF)
        g = jnp.where(block < i_f, gates[rows], NEG)
        mask = jnp.full(g.shape, NEG, F32)
        for _ in range(MOBA_TOPK):
            m = jnp.max(g, axis=0, keepdims=True)
            first = jnp.min(jnp.where(g == m, block, no_block), axis=0, keepdims=True)
            pick = block == first
            mask = jnp.where(pick, jnp.where(g > REAL_SCORE_FLOOR, 0.0, mask), mask)
            g = jnp.where(pick, NEG, g)
        halves = [(qt[rows] * scale).astype(BF16), mask.astype(BF16)]
        if h % 2:
            halves.reverse()
        qaug_ref[h * LANES:(h + 1) * LANES, :] = jnp.concatenate(halves, axis=0)


def _select(qt, gmat):
    rows = qt.shape[1]
    g_hi = gmat.astype(BF16)
    g_lo = (gmat - g_hi.astype(F32)).astype(BF16)
    return pl.pallas_call(
        _select_kernel,
        out_shape=jax.ShapeDtypeStruct((N_HEADS * LANES, rows), BF16),
        grid=(rows // MOBA_BLOCK,),
        in_specs=[pl.BlockSpec((ATTN_WIDTH, MOBA_BLOCK), lambda i: (0, i)),
                  _const_spec(gmat.shape), _const_spec(gmat.shape)],
        out_specs=pl.BlockSpec((N_HEADS * LANES, MOBA_BLOCK), lambda i: (0, i)),
        compiler_params=pltpu.CompilerParams(
            dimension_semantics=("arbitrary",), vmem_limit_bytes=VMEM_LIMIT_BYTES),
        name="moba_select",
    )(qt, g_hi, g_lo)


def _moba_attn_kernel(qa_ref, ka_ref, vt_ref, vt_own_ref, o_ref):
    i = pl.program_id(1)
    blk = qa_ref.shape[1]
    tile = vt_ref.shape[2]
    blocks_per_tile = tile // blk
    qs = [qa_ref[p * LANES:(p + 1) * LANES, :] for p in range(2)]

    def scores(k_rows, qs):
        return [_dot(k_rows[:, p * LANES:(p + 1) * LANES], qs[p]) for p in range(2)]

    ones_rows = jnp.ones((SUM_ROWS, blk), BF16)

    def absorb(state, s_pair, vt_cols):
        new = []
        for p in range(2):
            m, acc = state[p]
            s = s_pair[p]
            m_new = jnp.maximum(m, jnp.max(s, axis=0, keepdims=True))
            alpha = jnp.exp2(m - m_new)
            e = jnp.exp2((s - m_new).astype(BF16))
            vt_aug = jnp.concatenate([vt_cols[p * HALF:(p + 1) * HALF], ones_rows], axis=0)
            acc = alpha * acc + _dot(vt_aug, e)
            new.append((m_new, acc))
        return tuple(new)

    row = lax.broadcasted_iota(jnp.int32, (LANES, blk), 0)
    q_plain = [jnp.where(row < HALF if p == 0 else row >= HALF, qs[p], jnp.zeros_like(qs[p]))
               for p in range(2)]
    key_le_query = (lax.broadcasted_iota(jnp.int32, (blk, blk), 0)
                    <= lax.broadcasted_iota(jnp.int32, (blk, blk), 1))
    init = tuple((jnp.full((1, blk), NEG, F32), jnp.zeros((HALF + SUM_ROWS, blk), F32)) for _ in range(2))
    s_own = scores(ka_ref[pl.ds(pl.multiple_of(i * blk, blk), blk), :], q_plain)
    state = absorb(init, [jnp.where(key_le_query, s, NEG) for s in s_own], vt_own_ref[...])

    def past_tile(t, state):
        s_tile = scores(ka_ref[pl.ds(pl.multiple_of(t * tile, tile), tile), :], qs)
        for g in range(blocks_per_tile):
            rows = slice(g * blk, (g + 1) * blk)
            state = absorb(state, [s[rows] for s in s_tile], vt_ref[t, :, rows])
        return state

    n_past_tiles = (i + blocks_per_tile - 1) // blocks_per_tile
    state = lax.fori_loop(0, n_past_tiles, past_tile, state)
    out_t = jnp.concatenate([acc[:HALF] / acc[HALF:HALF + 1] for _, acc in state], axis=0)
    o_ref[...] = out_t.T.astype(BF16)


def _moba_attn(qaug_t, kaug, vtb):
    rows = kaug.shape[0]
    n_tiles = rows // KV_TILE
    blocks_per_tile = KV_TILE // MOBA_BLOCK
    return pl.pallas_call(
        _moba_attn_kernel,
        out_shape=jax.ShapeDtypeStruct((rows, ATTN_WIDTH), BF16),
        grid=(N_PAIRS, rows // MOBA_BLOCK),
        in_specs=[
            pl.BlockSpec((2 * LANES, MOBA_BLOCK), lambda c, i: (c, i)),
            pl.BlockSpec((rows, 2 * LANES), lambda c, i: (0, c)),
            pl.BlockSpec((n_tiles, LANES, KV_TILE), lambda c, i: (0, c, 0)),
            pl.BlockSpec((None, LANES, MOBA_BLOCK), lambda c, i: (i // blocks_per_tile, c, i % blocks_per_tile)),
        ],
        out_specs=pl.BlockSpec((MOBA_BLOCK, LANES), lambda c, i: (i, c)),
        compiler_params=pltpu.CompilerParams(
            dimension_semantics=("arbitrary", "arbitrary"), vmem_limit_bytes=VMEM_LIMIT_BYTES),
        name="moba_prompt_attn",
    )(qaug_t, kaug, vtb, vtb)


def _gate_matrices(kmean):
    n_blocks = kmean.shape[0]
    km = kmean.reshape(n_blocks, N_HEADS, HEAD_DIM).transpose(1, 0, 2)
    km = jnp.pad(km, ((0, 0), (0, HALF - n_blocks), (0, 0)))
    same_head = jnp.eye(N_HEADS, dtype=F32)
    return (km[:, :, None, :] * same_head[:, None, :, None]).reshape(N_HEADS * HALF, ATTN_WIDTH)


def _page_index(b, s, pt_ref, *, layer, slot):
    return (layer, pt_ref[b, s * PAGES_PER_STEP + slot % PAGES_PER_STEP], 0, 0, 0)


def _head_diagonal(x, rows_per_head):
    return jnp.concatenate(
        [x[h * rows_per_head:(h + 1) * rows_per_head, (h // 2) * LANES:(h // 2 + 1) * LANES]
         for h in range(N_HEADS)], axis=0)


def _sample_attn_kernel(pt_ref, qbd_ref, knew_ref, vnew_ref, ownbias_ref, *rest):
    del pt_ref
    k_refs = rest[:PAGES_PER_STEP]
    v_refs = rest[PAGES_PER_STEP:2 * PAGES_PER_STEP]
    o_ref, m_ref, l_ref, g_ref, part_ref = rest[2 * PAGES_PER_STEP:]
    s = pl.program_id(1)
    n_pages = part_ref.shape[0]
    rows = qbd_ref.shape[1]
    tokens = rows // N_HEADS
    lane = lax.broadcasted_iota(jnp.int32, (rows, LANES), 1)
    lane_f = lane.astype(F32)
    qb = (qbd_ref[0] * HEAD_DIM ** -0.5).astype(BF16)

    @pl.when(s == 0)
    def _():
        m_ref[...] = jnp.full(m_ref.shape, NEG, F32)
        l_ref[...] = jnp.zeros(l_ref.shape, F32)
        g_ref[...] = jnp.zeros(g_ref.shape, F32)

    first_page = s * PAGES_PER_STEP
    scores = [_dot(qb, k_refs[slot][...].reshape(ATTN_WIDTH, PAGE_SIZE).astype(BF16))
              for slot in range(PAGES_PER_STEP)]
    m_all, l_all, g_all = m_ref[...], l_ref[...], g_ref[...]
    probs = []
    for slot, sc in enumerate(scores):
        m = jnp.max(sc, axis=-1, keepdims=True)
        p = jnp.exp(sc - m)
        here = lane == first_page + slot
        m_all = jnp.where(here, m, m_all)
        l_all = jnp.where(here, jnp.sum(p, axis=-1, keepdims=True), l_all)
        g_all = jnp.where(here, jnp.sum(sc, axis=-1, keepdims=True), g_all)
        probs.append(p.astype(BF16))
    m_ref[...], l_ref[...], g_ref[...] = m_all, l_all, g_all
    pair_rows = 2 * tokens
    for slot, pb in enumerate(probs):
        vt = v_refs[slot][...].reshape(ATTN_WIDTH, PAGE_SIZE).astype(BF16)
        part_ref[first_page + slot] = jnp.concatenate(
            [_dot_nt(pb[c * pair_rows:(c + 1) * pair_rows], vt[c * LANES:(c + 1) * LANES])
             for c in range(N_PAIRS)], axis=0)

    @pl.when(s == pl.num_programs(1) - 1)
    def _():
        gate = g_ref[...]
        even = jnp.bitwise_and(lane, 1) == 0
        gate = gate + jnp.where(even, pltpu.roll(gate, n_pages - 1, 1), pltpu.roll(gate, 1, 1))
        block_first_lane = (lane - jnp.bitwise_and(lane, 1)).astype(F32)
        keep = jnp.zeros(gate.shape, jnp.bool_)
        for _ in range(MOBA_TOPK):
            gm = jnp.max(gate, axis=-1, keepdims=True)
            first = jnp.min(jnp.where(gate == gm, lane_f, 2.0 * LANES), axis=-1, keepdims=True)
            pick = block_first_lane == first
            keep = keep | pick
            gate = jnp.where(pick, NEG, gate)
        m_all = m_ref[...]
        s_own = _dot_nt(qb, knew_ref[0].astype(BF16)) + ownbias_ref[...]
        m_fin = jnp.maximum(jnp.max(jnp.where(keep, m_all, NEG), axis=-1, keepdims=True),
                            jnp.max(s_own, axis=-1, keepdims=True))
        p_own = jnp.exp(s_own - m_fin)
        w = jnp.where(keep, jnp.exp(m_all - m_fin), 0.0)
        l_fin = jnp.sum(p_own, axis=-1, keepdims=True) + jnp.sum(w * l_ref[...], axis=-1, keepdims=True)
        acc = _head_diagonal(_dot(p_own.astype(BF16), vnew_ref[0].astype(BF16)), tokens)
        for page in range(n_pages):
            acc = acc + w[:, page:page + 1] * part_ref[page]
        acc = acc / l_fin
        row = lax.broadcasted_iota(jnp.int32, (rows, LANES), 0)
        odd_head = jnp.bitwise_and(row, tokens) == tokens
        o_ref[0] = jnp.where(odd_head, pltpu.roll(acc, HALF, 1), acc)[:, :HEAD_DIM]


def _sample_attn(q, k_new, v_new, cache_kt, cache_vt, page_table, layer):
    n_seq, tokens, _ = q.shape
    n_pages = page_table.shape[1]
    rows = N_HEADS * tokens
    assert n_pages == LANES and MOBA_BLOCK == 2 * PAGE_SIZE
    assert tokens <= LANES and tokens & (tokens - 1) == 0
    assert n_pages * PAGE_SIZE // MOBA_BLOCK >= MOBA_TOPK
    q4 = q.reshape(n_seq, tokens, N_HEADS, HEAD_DIM).transpose(0, 2, 1, 3)
    qbd = (q4[:, :, :, None, :] * jnp.eye(N_HEADS, dtype=F32)[None, :, None, :, None]).reshape(
        n_seq, rows, ATTN_WIDTH)
    pad = ((0, 0), (0, LANES - tokens), (0, 0))
    k_new = jnp.pad(k_new, pad)
    v_new = jnp.pad(v_new, pad)
    t_q = jnp.arange(rows)[:, None] % tokens
    ownbias = jnp.where(jnp.arange(LANES)[None, :] <= t_q, 0.0, NEG).astype(F32)
    page_block = (None, None, N_HEADS, HEAD_DIM, PAGE_SIZE)
    page_specs = [pl.BlockSpec(page_block, functools.partial(_page_index, layer=layer, slot=slot))
                  for slot in range(2 * PAGES_PER_STEP)]
    grid_spec = pltpu.PrefetchScalarGridSpec(
        num_scalar_prefetch=1,
        grid=(n_seq, n_pages // PAGES_PER_STEP),
        in_specs=[
            pl.BlockSpec((1, rows, ATTN_WIDTH), lambda b, s, pt: (b, 0, 0)),
            pl.BlockSpec((1, LANES, ATTN_WIDTH), lambda b, s, pt: (b, 0, 0)),
            pl.BlockSpec((1, LANES, ATTN_WIDTH), lambda b, s, pt: (b, 0, 0)),
            pl.BlockSpec(ownbias.shape, lambda b, s, pt: (0, 0)),
        ] + page_specs,
        out_specs=pl.BlockSpec((1, rows, HEAD_DIM), lambda b, s, pt: (b, 0, 0)),
        scratch_shapes=[
            pltpu.VMEM((rows, n_pages), F32),
            pltpu.VMEM((rows, n_pages), F32),
            pltpu.VMEM((rows, n_pages), F32),
            pltpu.VMEM((n_pages, rows, LANES), F32),
        ],
    )
    return pl.pallas_call(
        _sample_attn_kernel,
        out_shape=jax.ShapeDtypeStruct((n_seq, rows, HEAD_DIM), F32),
        grid_spec=grid_spec,
        compiler_params=pltpu.CompilerParams(
            dimension_semantics=("arbitrary", "arbitrary"), vmem_limit_bytes=VMEM_LIMIT_BYTES),
        name="moba_sample_attn",
    )(page_table, qbd, k_new, v_new, ownbias,
      *([cache_kt] * PAGES_PER_STEP), *([cache_vt] * PAGES_PER_STEP))


def _post_kernel(x_ref, a_ref, b_ref, gpre_ref, wg_ref, wpa_ref, wpb_ref, wout_ref,
                 gpm_ref, gpf_ref, wup_ref, wdown_ref, gff_ref, o_ref):
    x = x_ref[...]
    d_model = x.shape[1]
    h = _rmsnorm(x, gpre_ref[...]).astype(BF16)
    gates = jax.nn.sigmoid(_dot(h, wg_ref[...]))
    merged = (gates[:, :d_model] * _dot(a_ref[...], wpa_ref[...])
              + gates[:, d_model:] * _dot(b_ref[...], wpb_ref[...]))
    x = x + _rmsnorm(_dot(merged.astype(BF16), wout_ref[...]), gpm_ref[...])
    h2 = _rmsnorm(x, gpf_ref[...]).astype(BF16)
    slab = wup_ref.shape[1] // FFN_SPLIT
    f = jnp.zeros(x.shape, F32)
    for c in range(FFN_SPLIT):
        up = _dot(h2, wup_ref[:, c * slab:(c + 1) * slab])
        f = f + _dot(jnp.square(jnp.maximum(up, 0.0)).astype(BF16), wdown_ref[c * slab:(c + 1) * slab, :])
    o_ref[...] = x + _rmsnorm(f, gff_ref[...])


def _post(x, a, b, g_pre, w_g, w_pa, w_pb, w_out, g_pm, g_pf, w_up, w_down, g_ff, *, tm):
    rows, d_model = x.shape
    consts = (g_pre, w_g, w_pa, w_pb, w_out, g_pm, g_pf, w_up, w_down, g_ff)
    return pl.pallas_call(
        _post_kernel,
        out_shape=jax.ShapeDtypeStruct((rows, d_model), F32),
        grid=(rows // tm,),
        in_specs=[_row_spec(tm, d_model), _row_spec(tm, SGU_WIDTH), _row_spec(tm, ATTN_WIDTH)]
        + [_const_spec(c.shape) for c in consts],
        out_specs=_row_spec(tm, d_model),
        compiler_params=pltpu.CompilerParams(
            dimension_semantics=("arbitrary",), vmem_limit_bytes=VMEM_LIMIT_BYTES),
        name="merge_mlp",
    )(x, a, b, *consts)


def _pair_weights(w):
    g, p, j = w.shape
    return w.reshape(g // 2, 2, p, j).transpose(0, 2, 1, 3).reshape(g // 2, p, 2 * j)


def kernel(x_prompt, x_sample, cache_k, cache_v, page_table, g_pre_mix, w_in, g_sgu, w_s, b_s, w_pa, w_pb,
           w_out, g_post_mix, g_pre_ffn, w_up, w_down, g_post_ffn):
    depth = w_in.shape[0]
    batch, seq, d_model = x_prompt.shape
    dec_batch, dec_seq, _ = x_sample.shape
    assert batch == 1 and seq % KV_TILE == 0 and KV_TILE % PROMPT_ROWS == 0 and seq // MOBA_BLOCK <= HALF
    assert dec_seq <= CHUNK and SGU_GROUP_DIM == HALF and HEAD_DIM == HALF
    n_qkvuz = 3 * ATTN_WIDTH + 2 * SGU_WIDTH
    dec_rows = dec_batch * dec_seq

    cache_kt = cache_k.transpose(0, 1, 3, 4, 2)
    cache_vt = cache_v.transpose(0, 1, 3, 4, 2)
    xp = x_prompt.reshape(seq, d_model)
    xs = x_sample.reshape(dec_rows, d_model)
    kp_rows, vp_rows, ks_rows, vs_rows, zs_rows = [], [], [], [], []
    for l in range(depth):
        row = lambda g: g[l][None, :]
        w_qkvuz = w_in[l][:, :n_qkvuz].astype(BF16)
        w_gates = w_in[l][:, n_qkvuz:].astype(BF16)
        post_w = (row(g_pre_mix), w_gates, w_pa[l].astype(BF16), w_pb[l].astype(BF16), w_out[l].astype(BF16),
                  row(g_post_mix), row(g_pre_ffn), w_up[l].astype(BF16), w_down[l].astype(BF16), row(g_post_ffn))

        bias_p = jnp.repeat(b_s[l].T, SGU_GROUP_DIM, axis=1)
        qt, a, kmean, kaug, kt, vt, vtb = _inproj(
            xp, row(g_pre_mix), w_qkvuz, row(g_sgu), _pair_weights(w_s[l]), bias_p,
            tm=PROMPT_ROWS, chunk=CHUNK, emit_attn=True)
        qaug_t = _select(qt, _gate_matrices(kmean.reshape(seq // MOBA_BLOCK, ATTN_WIDTH)))
        b_attn = _moba_attn(qaug_t, kaug, vtb)
        xp = _post(xp, a, b_attn, *post_w, tm=PROMPT_ROWS)
        kp_rows.append(kt.reshape(N_HEADS, HEAD_DIM, seq).transpose(2, 0, 1)[None])
        vp_rows.append(vt.reshape(N_HEADS, HEAD_DIM, seq).transpose(2, 0, 1)[None])

        w_blockdiag = (jnp.eye(dec_batch, dtype=F32)[None, :, None, :, None]
                       * w_s[l][:, None, :dec_seq, None, :dec_seq]).reshape(SGU_GROUPS, dec_rows, dec_rows)
        bias_s = jnp.tile(bias_p[:dec_seq], (dec_batch, 1))
        q, a, k, v, z = _inproj(
            xs, row(g_pre_mix), w_qkvuz, row(g_sgu), _pair_weights(w_blockdiag), bias_s,
            tm=dec_rows, chunk=dec_rows, emit_attn=False)
        o = _sample_attn(q.reshape(dec_batch, dec_seq, ATTN_WIDTH), k.reshape(dec_batch, dec_seq, ATTN_WIDTH),
                         v.reshape(dec_batch, dec_seq, ATTN_WIDTH), cache_kt, cache_vt, page_table, l)
        b_attn = o.reshape(dec_batch, N_HEADS, dec_seq, HEAD_DIM).transpose(0, 2, 1, 3).reshape(
            dec_rows, ATTN_WIDTH).astype(BF16)
        xs = _post(xs, a, b_attn, *post_w, tm=dec_rows)
        ks_rows.append(k.reshape(dec_batch, dec_seq, N_HEADS, HEAD_DIM))
        vs_rows.append(v.reshape(dec_batch, dec_seq, N_HEADS, HEAD_DIM))
        zs_rows.append(z.reshape(dec_batch, dec_seq, SGU_WIDTH))

    return (xp.reshape(batch, seq, d_model), xs.reshape(dec_batch, dec_seq, d_model),
            jnp.stack(kp_rows), jnp.stack(vp_rows), jnp.stack(ks_rows), jnp.stack(vs_rows), jnp.stack(zs_rows))
```

```python
import functools

import jax
import jax.numpy as jnp
from jax import lax
from jax.experimental import pallas as pl
from jax.experimental.pallas import tpu as pltpu

N_HEADS = 8
HEAD_DIM = 64
ATTN_WIDTH = N_HEADS * HEAD_DIM
SGU_GROUPS = 8
SGU_WIDTH = 512
SGU_GROUP_DIM = SGU_WIDTH // SGU_GROUPS
CHUNK = 128
MOBA_BLOCK = 256
MOBA_TOPK = 3
PAGE_SIZE = 128
EPS = 1e-6
NEG = -1e30
REAL_SCORE_FLOOR = -1e29
LOG2_E = 1.4426950408889634

LANES = 128
HALF = LANES // 2
N_PAIRS = N_HEADS // 2
VMEM_LIMIT_BYTES = 56 * 1024 * 1024

PROMPT_ROWS = 512
FFN_SPLIT = 4
SUM_ROWS = 16
Q_BLOCKS_PER_STEP = 2
KV_TILE = 1024
PAGES_PER_STEP = 32

F32 = jnp.float32
BF16 = jnp.bfloat16


def _rmsnorm(x, g):
    return x * lax.rsqrt(jnp.mean(x * x, axis=-1, keepdims=True) + EPS) * g


def _dot(a, b):
    return jnp.dot(a, b, preferred_element_type=F32)


def _dot_nt(a, b):
    return lax.dot_general(a, b, (((1,), (1,)), ((), ())), preferred_element_type=F32)


def _const_spec(shape):
    zeros = (0,) * len(shape)
    return pl.BlockSpec(shape, lambda *_: zeros, pipeline_mode=pl.Buffered(1))


def _row_spec(rows, cols):
    return pl.BlockSpec((rows, cols), lambda i: (i, 0))


def _inproj_kernel(x_ref, gpre_ref, w_ref, gsgu_ref, ws_ref, bias_ref, *out_refs, chunk, emit_attn):
    if emit_attn:
        q_ref, a_ref, kmean_ref, kaug_ref, kt_ref, vt_ref, vtb_ref = out_refs
    else:
        q_ref, a_ref, k_ref, v_ref, z_ref = out_refs
    tm = x_ref.shape[0]
    h = _rmsnorm(x_ref[...], gpre_ref[...]).astype(BF16)
    y = _dot(h, w_ref[...])
    q = y[:, 0:ATTN_WIDTH]
    k = y[:, ATTN_WIDTH:2 * ATTN_WIDTH]
    v = y[:, 2 * ATTN_WIDTH:3 * ATTN_WIDTH]
    u = y[:, 3 * ATTN_WIDTH:3 * ATTN_WIDTH + SGU_WIDTH]
    z = y[:, 3 * ATTN_WIDTH + SGU_WIDTH:]
    q_ref[...] = q.T if emit_attn else q
    zn = _rmsnorm(jax.nn.gelu(z), gsgu_ref[...])
    if not emit_attn:
        k_ref[...] = k
        v_ref[...] = v
        z_ref[...] = zn

    n_chunks = tm // chunk
    low = lax.broadcasted_iota(jnp.int32, (chunk, LANES), 1) < HALF
    w_row = lax.broadcasted_iota(jnp.int32, (chunk, 2 * chunk), 0)
    w_col = lax.broadcasted_iota(jnp.int32, (chunk, 2 * chunk), 1)
    causal = jnp.where(w_col >= chunk, w_col - chunk, w_col) <= w_row
    pair_out = []
    for p in range(N_PAIRS):
        w = jnp.where(causal, ws_ref[p], 0.0).astype(BF16)
        cols = []
        for c in range(n_chunks):
            zc = zn[c * chunk:(c + 1) * chunk, p * LANES:(p + 1) * LANES]
            cols.append(jnp.concatenate([jnp.where(low, zc, 0.0), jnp.where(low, 0.0, zc)], axis=0))
        rhs = jnp.concatenate(cols, axis=1).astype(BF16)
        pair_out.append(_dot(w, rhs))
    bias = bias_ref[...]
    rows = []
    for c in range(n_chunks):
        rows.append(jnp.concatenate([po[:, c * LANES:(c + 1) * LANES] for po in pair_out], axis=1) + bias)
    s = jnp.concatenate(rows, axis=0)
    a_ref[...] = (jax.nn.gelu(u) * s).astype(BF16)

    if emit_attn:
        kmean_ref[0] = jnp.mean(k.reshape(tm // MOBA_BLOCK, MOBA_BLOCK, ATTN_WIDTH), axis=1)
        row_g = pl.program_id(0) * tm + lax.broadcasted_iota(jnp.int32, (tm, LANES), 0)
        blk = jnp.right_shift(row_g, MOBA_BLOCK.bit_length() - 1)
        lane = lax.broadcasted_iota(jnp.int32, (tm, LANES), 1)
        low_t = lane < HALF
        onehot_hi = jnp.where(lane - HALF == blk, 1.0, 0.0)
        onehot_lo = jnp.where(lane == blk, 1.0, 0.0)
        for p in range(N_PAIRS):
            kc = k[:, p * LANES:(p + 1) * LANES]
            kaug_ref[:, (2 * p) * LANES:(2 * p + 1) * LANES] = jnp.where(low_t, kc, onehot_hi).astype(BF16)
            kaug_ref[:, (2 * p + 1) * LANES:(2 * p + 2) * LANES] = jnp.where(low_t, onehot_lo, kc).astype(BF16)
        kt_ref[...] = k.T
        vt = v.T
        vt_ref[...] = vt
        vtb_ref[0] = vt.astype(BF16)


def _inproj(x, g_pre, w_qkvuz, g_sgu, ws_pairs, bias, *, tm, chunk, emit_attn):
    rows, d_model = x.shape
    n_tiles = rows // tm
    out_shape = [jax.ShapeDtypeStruct((rows, ATTN_WIDTH), F32), jax.ShapeDtypeStruct((rows, SGU_WIDTH), BF16)]
    out_specs = [_row_spec(tm, ATTN_WIDTH), _row_spec(tm, SGU_WIDTH)]
    if emit_attn:
        out_shape[0] = jax.ShapeDtypeStruct((ATTN_WIDTH, rows), F32)
        out_specs[0] = pl.BlockSpec((ATTN_WIDTH, tm), lambda i: (0, i))
        blocks_per_tile = tm // MOBA_BLOCK
        out_shape += [
            jax.ShapeDtypeStruct((n_tiles, blocks_per_tile, ATTN_WIDTH), F32),
            jax.ShapeDtypeStruct((rows, N_HEADS * LANES), BF16),
            jax.ShapeDtypeStruct((ATTN_WIDTH, rows), F32),
            jax.ShapeDtypeStruct((ATTN_WIDTH, rows), F32),
            jax.ShapeDtypeStruct((rows // KV_TILE, ATTN_WIDTH, KV_TILE), BF16),
        ]
        tiles_per_kv = KV_TILE // tm
        out_specs += [
            pl.BlockSpec((1, blocks_per_tile, ATTN_WIDTH), lambda i: (i, 0, 0)),
            _row_spec(tm, N_HEADS * LANES),
            pl.BlockSpec((ATTN_WIDTH, tm), lambda i: (0, i)),
            pl.BlockSpec((ATTN_WIDTH, tm), lambda i: (0, i)),
            pl.BlockSpec((1, ATTN_WIDTH, tm), lambda i: (i // tiles_per_kv, 0, i % tiles_per_kv)),
        ]
    else:
        out_shape += [jax.ShapeDtypeStruct((rows, ATTN_WIDTH), F32)] * 2 + [
            jax.ShapeDtypeStruct((rows, SGU_WIDTH), F32)]
        out_specs += [_row_spec(tm, ATTN_WIDTH)] * 2 + [_row_spec(tm, SGU_WIDTH)]
    return pl.pallas_call(
        functools.partial(_inproj_kernel, chunk=chunk, emit_attn=emit_attn),
        out_shape=out_shape,
        grid=(n_tiles,),
        in_specs=[
            _row_spec(tm, d_model),
            _const_spec(g_pre.shape),
            _const_spec(w_qkvuz.shape),
            _const_spec(g_sgu.shape),
            _const_spec(ws_pairs.shape),
            _const_spec(bias.shape),
        ],
        out_specs=out_specs,
        compiler_params=pltpu.CompilerParams(
            dimension_semantics=("arbitrary",), vmem_limit_bytes=VMEM_LIMIT_BYTES),
        name="inproj_attn" if emit_attn else "inproj",
    )(x, g_pre, w_qkvuz, g_sgu, ws_pairs, bias)


def _select_kernel(qt_ref, ghi_ref, glo_ref, qaug_ref):
    i_f = pl.program_id(0).astype(F32)
    n_q = qt_ref.shape[1]
    scale = HEAD_DIM ** -0.5 * LOG2_E
    qt = qt_ref[...]
    q_hi = qt.astype(BF16)
    q_lo = (qt - q_hi.astype(F32)).astype(BF16)
    gates = _dot(ghi_ref[...], q_hi) + (_dot(ghi_ref[...], q_lo) + _dot(glo_ref[...], q_hi))
    block = lax.broadcasted_iota(jnp.int32, (HALF, n_q), 0).astype(F32)
    no_block = 2.0 * LANES
    for h in range(N_HEADS):
        rows = slice(h * HALF, (h + 1) * HALF)
        g = jnp.where(block < i_f, gates[rows], NEG)
        mask = jnp.full(g.shape, NEG, F32)
        for _ in range(MOBA_TOPK):
            m = jnp.max(g, axis=0, keepdims=True)
            first = jnp.min(jnp.where(g == m, block, no_block), axis=0, keepdims=True)
            pick = block == first
            mask = jnp.where(pick, jnp.where(g > REAL_SCORE_FLOOR, 0.0, mask), mask)
            g = jnp.where(pick, NEG, g)
        halves = [(qt[rows] * scale).astype(BF16), mask.astype(BF16)]
        if h % 2:
            halves.reverse()
        qaug_ref[h * LANES:(h + 1) * LANES, :] = jnp.concatenate(halves, axis=0)


def _select(qt, gmat):
    rows = qt.shape[1]
    g_hi = gmat.astype(BF16)
    g_lo = (gmat - g_hi.astype(F32)).astype(BF16)
    return pl.pallas_call(
        _select_kernel,
        out_shape=jax.ShapeDtypeStruct((N_HEADS * LANES, rows), BF16),
        grid=(rows // MOBA_BLOCK,),
        in_specs=[pl.BlockSpec((ATTN_WIDTH, MOBA_BLOCK), lambda i: (0, i)),
                  _const_spec(gmat.shape), _const_spec(gmat.shape)],
        out_specs=pl.BlockSpec((N_HEADS * LANES, MOBA_BLOCK), lambda i: (0, i)),
        compiler_params=pltpu.CompilerParams(
            dimension_semantics=("arbitrary",), vmem_limit_bytes=VMEM_LIMIT_BYTES),
        name="moba_select",
    )(qt, g_hi, g_lo)


def _moba_attn_kernel(qa_ref, ka_ref, vt_ref, vt_own_ref, o_ref):
    step = pl.program_id(1)
    blk = MOBA_BLOCK
    n_qb = qa_ref.shape[1] // blk
    tile = vt_ref.shape[2]
    blocks_per_tile = tile // blk
    streams = [(qb, p) for qb in range(n_qb) for p in range(2)]
    qs = [qa_ref[p * LANES:(p + 1) * LANES, qb * blk:(qb + 1) * blk] for qb, p in streams]

    def scores(k_rows, qs, which):
        return [_dot(k_rows[:, p * LANES:(p + 1) * LANES], qs[n]) if n in which else None
                for n, (_, p) in enumerate(streams)]

    ones_rows = jnp.ones((SUM_ROWS, blk), BF16)

    def absorb(state, s_list, vt_cols):
        new = []
        for n, (_, p) in enumerate(streams):
            m, acc = state[n]
            s = s_list[n]
            if s is not None:
                m_new = jnp.maximum(m, jnp.max(s, axis=0, keepdims=True))
                alpha = jnp.exp2(m - m_new)
                e = jnp.exp2((s - m_new).astype(BF16))
                vt_aug = jnp.concatenate([vt_cols[p * HALF:(p + 1) * HALF], ones_rows], axis=0)
                m, acc = m_new, alpha * acc + _dot(vt_aug, e)
            new.append((m, acc))
        return tuple(new)

    row = lax.broadcasted_iota(jnp.int32, (LANES, blk), 0)
    q_plain = [jnp.where(row < HALF if p == 0 else row >= HALF, q, jnp.zeros_like(q))
               for q, (_, p) in zip(qs, streams)]
    key_le_query = (lax.broadcasted_iota(jnp.int32, (blk, blk), 0)
                    <= lax.broadcasted_iota(jnp.int32, (blk, blk), 1))
    state = tuple((jnp.full((1, blk), NEG, F32), jnp.zeros((HALF + SUM_ROWS, blk), F32)) for _ in streams)
    for qb in range(n_qb):
        mine = [n for n, (b, _) in enumerate(streams) if b == qb]
        own = pl.multiple_of((step * n_qb + qb) * blk, blk)
        s_own = scores(ka_ref[pl.ds(own, blk), :], q_plain, mine)
        s_own = [None if s is None else jnp.where(key_le_query, s, NEG) for s in s_own]
        state = absorb(state, s_own, vt_own_ref[:, qb * blk:(qb + 1) * blk])

    everyone = range(len(streams))

    def past_tile(t, state):
        s_tile = scores(ka_ref[pl.ds(pl.multiple_of(t * tile, tile), tile), :], qs, everyone)
        for g in range(blocks_per_tile):
            rows = slice(g * blk, (g + 1) * blk)
            state = absorb(state, [s[rows] for s in s_tile], vt_ref[t, :, rows])
        return state

    latest_block = step * n_qb + n_qb - 1
    state = lax.fori_loop(0, (latest_block + blocks_per_tile - 1) // blocks_per_tile, past_tile, state)
    for qb in range(n_qb):
        out_t = jnp.concatenate([state[n][1][:HALF] / state[n][1][HALF:HALF + 1]
                                 for n, (b, _) in enumerate(streams) if b == qb], axis=0)
        o_ref[qb * blk:(qb + 1) * blk, :] = out_t.T.astype(BF16)


def _moba_attn(qaug_t, kaug, vtb):
    rows = kaug.shape[0]
    n_tiles = rows // KV_TILE
    q_rows = Q_BLOCKS_PER_STEP * MOBA_BLOCK
    steps_per_tile = KV_TILE // q_rows
    return pl.pallas_call(
        _moba_attn_kernel,
        out_shape=jax.ShapeDtypeStruct((rows, ATTN_WIDTH), BF16),
        grid=(N_PAIRS, rows // q_rows),
        in_specs=[
            pl.BlockSpec((2 * LANES, q_rows), lambda c, i: (c, i)),
            pl.BlockSpec((rows, 2 * LANES), lambda c, i: (0, c)),
            pl.BlockSpec((n_tiles, LANES, KV_TILE), lambda c, i: (0, c, 0)),
            pl.BlockSpec((None, LANES, q_rows), lambda c, i: (i // steps_per_tile, c, i % steps_per_tile)),
        ],
        out_specs=pl.BlockSpec((q_rows, LANES), lambda c, i: (i, c)),
        compiler_params=pltpu.CompilerParams(
            dimension_semantics=("arbitrary", "arbitrary"), vmem_limit_bytes=VMEM_LIMIT_BYTES),
        name="moba_prompt_attn",
    )(qaug_t, kaug, vtb, vtb)


def _gate_matrices(kmean):
    n_blocks = kmean.shape[0]
    km = kmean.reshape(n_blocks, N_HEADS, HEAD_DIM).transpose(1, 0, 2)
    km = jnp.pad(km, ((0, 0), (0, HALF - n_blocks), (0, 0)))
    same_head = jnp.eye(N_HEADS, dtype=F32)
    return (km[:, :, None, :] * same_head[:, None, :, None]).reshape(N_HEADS * HALF, ATTN_WIDTH)


def _page_index(b, s, pt_ref, *, layer, slot):
    return (layer, pt_ref[b, s * PAGES_PER_STEP + slot % PAGES_PER_STEP], 0, 0, 0)


def _head_diagonal(x, rows_per_head):
    return jnp.concatenate(
        [x[h * rows_per_head:(h + 1) * rows_per_head, (h // 2) * LANES:(h // 2 + 1) * LANES]
         for h in range(N_HEADS)], axis=0)


def _sample_attn_kernel(pt_ref, qbd_ref, knew_ref, vnew_ref, ownbias_ref, *rest):
    del pt_ref
    k_refs = rest[:PAGES_PER_STEP]
    v_refs = rest[PAGES_PER_STEP:2 * PAGES_PER_STEP]
    o_ref, m_ref, l_ref, g_ref, part_ref = rest[2 * PAGES_PER_STEP:]
    s = pl.program_id(1)
    n_pages = part_ref.shape[0]
    rows = qbd_ref.shape[1]
    tokens = rows // N_HEADS
    lane = lax.broadcasted_iota(jnp.int32, (rows, LANES), 1)
    lane_f = lane.astype(F32)
    qb = (qbd_ref[0] * HEAD_DIM ** -0.5).astype(BF16)

    @pl.when(s == 0)
    def _():
        m_ref[...] = jnp.full(m_ref.shape, NEG, F32)
        l_ref[...] = jnp.zeros(l_ref.shape, F32)
        g_ref[...] = jnp.zeros(g_ref.shape, F32)

    first_page = s * PAGES_PER_STEP
    scores = [_dot(qb, k_refs[slot][...].reshape(ATTN_WIDTH, PAGE_SIZE).astype(BF16))
              for slot in range(PAGES_PER_STEP)]
    m_all, l_all, g_all = m_ref[...], l_ref[...], g_ref[...]
    probs = []
    for slot, sc in enumerate(scores):
        m = jnp.max(sc, axis=-1, keepdims=True)
        p = jnp.exp(sc - m)
        here = lane == first_page + slot
        m_all = jnp.where(here, m, m_all)
        l_all = jnp.where(here, jnp.sum(p, axis=-1, keepdims=True), l_all)
        g_all = jnp.where(here, jnp.sum(sc, axis=-1, keepdims=True), g_all)
        probs.append(p.astype(BF16))
    m_ref[...], l_ref[...], g_ref[...] = m_all, l_all, g_all
    pair_rows = 2 * tokens
    for slot, pb in enumerate(probs):
        vt = v_refs[slot][...].reshape(ATTN_WIDTH, PAGE_SIZE).astype(BF16)
        part_ref[first_page + slot] = jnp.concatenate(
            [_dot_nt(pb[c * pair_rows:(c + 1) * pair_rows], vt[c * LANES:(c + 1) * LANES])
             for c in range(N_PAIRS)], axis=0)

    @pl.when(s == pl.num_programs(1) - 1)
    def _():
        gate = g_ref[...]
        even = jnp.bitwise_and(lane, 1) == 0
        gate = gate + jnp.where(even, pltpu.roll(gate, n_pages - 1, 1), pltpu.roll(gate, 1, 1))
        block_first_lane = (lane - jnp.bitwise_and(lane, 1)).astype(F32)
        keep = jnp.zeros(gate.shape, jnp.bool_)
        for _ in range(MOBA_TOPK):
            gm = jnp.max(gate, axis=-1, keepdims=True)
            first = jnp.min(jnp.where(gate == gm, lane_f, 2.0 * LANES), axis=-1, keepdims=True)
            pick = block_first_lane == first
            keep = keep | pick
            gate = jnp.where(pick, NEG, gate)
        m_all = m_ref[...]
        s_own = _dot_nt(qb, knew_ref[0].astype(BF16)) + ownbias_ref[...]
        m_fin = jnp.maximum(jnp.max(jnp.where(keep, m_all, NEG), axis=-1, keepdims=True),
                            jnp.max(s_own, axis=-1, keepdims=True))
        p_own = jnp.exp(s_own - m_fin)
        w = jnp.where(keep, jnp.exp(m_all - m_fin), 0.0)
        l_fin = jnp.sum(p_own, axis=-1, keepdims=True) + jnp.sum(w * l_ref[...], axis=-1, keepdims=True)
        acc = _head_diagonal(_dot(p_own.astype(BF16), vnew_ref[0].astype(BF16)), tokens)
        for page in range(n_pages):
            acc = acc + w[:, page:page + 1] * part_ref[page]
        acc = acc / l_fin
        row = lax.broadcasted_iota(jnp.int32, (rows, LANES), 0)
        odd_head = jnp.bitwise_and(row, tokens) == tokens
        o_ref[0] = jnp.where(odd_head, pltpu.roll(acc, HALF, 1), acc)[:, :HEAD_DIM]


def _sample_attn(q, k_new, v_new, cache_kt, cache_vt, page_table, layer):
    n_seq, tokens, _ = q.shape
    n_pages = page_table.shape[1]
    rows = N_HEADS * tokens
    assert n_pages == LANES and MOBA_BLOCK == 2 * PAGE_SIZE
    assert tokens <= LANES and tokens & (tokens - 1) == 0
    assert n_pages * PAGE_SIZE // MOBA_BLOCK >= MOBA_TOPK
    q4 = q.reshape(n_seq, tokens, N_HEADS, HEAD_DIM).transpose(0, 2, 1, 3)
    qbd = (q4[:, :, :, None, :] * jnp.eye(N_HEADS, dtype=F32)[None, :, None, :, None]).reshape(
        n_seq, rows, ATTN_WIDTH)
    pad = ((0, 0), (0, LANES - tokens), (0, 0))
    k_new = jnp.pad(k_new, pad)
    v_new = jnp.pad(v_new, pad)
    t_q = jnp.arange(rows)[:, None] % tokens
    ownbias = jnp.where(jnp.arange(LANES)[None, :] <= t_q, 0.0, NEG).astype(F32)
    page_block = (None, None, N_HEADS, HEAD_DIM, PAGE_SIZE)
    page_specs = [pl.BlockSpec(page_block, functools.partial(_page_index, layer=layer, slot=slot))
                  for slot in range(2 * PAGES_PER_STEP)]
    grid_spec = pltpu.PrefetchScalarGridSpec(
        num_scalar_prefetch=1,
        grid=(n_seq, n_pages // PAGES_PER_STEP),
        in_specs=[
            pl.BlockSpec((1, rows, ATTN_WIDTH), lambda b, s, pt: (b, 0, 0)),
            pl.BlockSpec((1, LANES, ATTN_WIDTH), lambda b, s, pt: (b, 0, 0)),
            pl.BlockSpec((1, LANES, ATTN_WIDTH), lambda b, s, pt: (b, 0, 0)),
            pl.BlockSpec(ownbias.shape, lambda b, s, pt: (0, 0)),
        ] + page_specs,
        out_specs=pl.BlockSpec((1, rows, HEAD_DIM), lambda b, s, pt: (b, 0, 0)),
        scratch_shapes=[
            pltpu.VMEM((rows, n_pages), F32),
            pltpu.VMEM((rows, n_pages), F32),
            pltpu.VMEM((rows, n_pages), F32),
            pltpu.VMEM((n_pages, rows, LANES), F32),
        ],
    )
    return pl.pallas_call(
        _sample_attn_kernel,
        out_shape=jax.ShapeDtypeStruct((n_seq, rows, HEAD_DIM), F32),
        grid_spec=grid_spec,
        compiler_params=pltpu.CompilerParams(
            dimension_semantics=("arbitrary", "arbitrary"), vmem_limit_bytes=VMEM_LIMIT_BYTES),
        name="moba_sample_attn",
    )(page_table, qbd, k_new, v_new, ownbias,
      *([cache_kt] * PAGES_PER_STEP), *([cache_vt] * PAGES_PER_STEP))


def _post_kernel(x_ref, a_ref, b_ref, gpre_ref, wg_ref, wpa_ref, wpb_ref, wout_ref,
                 gpm_ref, gpf_ref, wup_ref, wdown_ref, gff_ref, o_ref):
    x = x_ref[...]
    d_model = x.shape[1]
    h = _rmsnorm(x, gpre_ref[...]).astype(BF16)
    gates = jax.nn.sigmoid(_dot(h, wg_ref[...]))
    merged = (gates[:, :d_model] * _dot(a_ref[...], wpa_ref[...])
              + gates[:, d_model:] * _dot(b_ref[...], wpb_ref[...]))
    x = x + _rmsnorm(_dot(merged.astype(BF16), wout_ref[...]), gpm_ref[...])
    h2 = _rmsnorm(x, gpf_ref[...]).astype(BF16)
    slab = wup_ref.shape[1] // FFN_SPLIT
    f = jnp.zeros(x.shape, F32)
    for c in range(FFN_SPLIT):
        up = _dot(h2, wup_ref[:, c * slab:(c + 1) * slab])
        f = f + _dot(jnp.square(jnp.maximum(up, 0.0)).astype(BF16), wdown_ref[c * slab:(c + 1) * slab, :])
    o_ref[...] = x + _rmsnorm(f, gff_ref[...])


def _post(x, a, b, g_pre, w_g, w_pa, w_pb, w_out, g_pm, g_pf, w_up, w_down, g_ff, *, tm):
    rows, d_model = x.shape
    consts = (g_pre, w_g, w_pa, w_pb, w_out, g_pm, g_pf, w_up, w_down, g_ff)
    return pl.pallas_call(
        _post_kernel,
        out_shape=jax.ShapeDtypeStruct((rows, d_model), F32),
        grid=(rows // tm,),
        in_specs=[_row_spec(tm, d_model), _row_spec(tm, SGU_WIDTH), _row_spec(tm, ATTN_WIDTH)]
        + [_const_spec(c.shape) for c in consts],
        out_specs=_row_spec(tm, d_model),
        compiler_params=pltpu.CompilerParams(
            dimension_semantics=("arbitrary",), vmem_limit_bytes=VMEM_LIMIT_BYTES),
        name="merge_mlp",
    )(x, a, b, *consts)


def _pair_weights(w):
    g, p, j = w.shape
    return w.reshape(g // 2, 2, p, j).transpose(0, 2, 1, 3).reshape(g // 2, p, 2 * j)


def kernel(x_prompt, x_sample, cache_k, cache_v, page_table, g_pre_mix, w_in, g_sgu, w_s, b_s, w_pa, w_pb,
           w_out, g_post_mix, g_pre_ffn, w_up, w_down, g_post_ffn):
    depth = w_in.shape[0]
    batch, seq, d_model = x_prompt.shape
    dec_batch, dec_seq, _ = x_sample.shape
    assert batch == 1 and seq % KV_TILE == 0 and KV_TILE % PROMPT_ROWS == 0 and seq // MOBA_BLOCK <= HALF
    assert dec_seq <= CHUNK and SGU_GROUP_DIM == HALF and HEAD_DIM == HALF
    n_qkvuz = 3 * ATTN_WIDTH + 2 * SGU_WIDTH
    dec_rows = dec_batch * dec_seq

    cache_kt = cache_k.transpose(0, 1, 3, 4, 2)
    cache_vt = cache_v.transpose(0, 1, 3, 4, 2)
    xp = x_prompt.reshape(seq, d_model)
    xs = x_sample.reshape(dec_rows, d_model)
    kp_rows, vp_rows, ks_rows, vs_rows, zs_rows = [], [], [], [], []
    for l in range(depth):
        row = lambda g: g[l][None, :]
        w_qkvuz = w_in[l][:, :n_qkvuz].astype(BF16)
        w_gates = w_in[l][:, n_qkvuz:].astype(BF16)
        post_w = (row(g_pre_mix), w_gates, w_pa[l].astype(BF16), w_pb[l].astype(BF16), w_out[l].astype(BF16),
                  row(g_post_mix), row(g_pre_ffn), w_up[l].astype(BF16), w_down[l].astype(BF16), row(g_post_ffn))

        bias_p = jnp.repeat(b_s[l].T, SGU_GROUP_DIM, axis=1)
        qt, a, kmean, kaug, kt, vt, vtb = _inproj(
            xp, row(g_pre_mix), w_qkvuz, row(g_sgu), _pair_weights(w_s[l]), bias_p,
            tm=PROMPT_ROWS, chunk=CHUNK, emit_attn=True)
        qaug_t = _select(qt, _gate_matrices(kmean.reshape(seq // MOBA_BLOCK, ATTN_WIDTH)))
        b_attn = _moba_attn(qaug_t, kaug, vtb)
        xp = _post(xp, a, b_attn, *post_w, tm=PROMPT_ROWS)
        kp_rows.append(kt.reshape(N_HEADS, HEAD_DIM, seq).transpose(2, 0, 1)[None])
        vp_rows.append(vt.reshape(N_HEADS, HEAD_DIM, seq).transpose(2, 0, 1)[None])

        w_blockdiag = (jnp.eye(dec_batch, dtype=F32)[None, :, None, :, None]
                       * w_s[l][:, None, :dec_seq, None, :dec_seq]).reshape(SGU_GROUPS, dec_rows, dec_rows)
        bias_s = jnp.tile(bias_p[:dec_seq], (dec_batch, 1))
        q, a, k, v, z = _inproj(
            xs, row(g_pre_mix), w_qkvuz, row(g_sgu), _pair_weights(w_blockdiag), bias_s,
            tm=dec_rows, chunk=dec_rows, emit_attn=False)
        o = _sample_attn(q.reshape(dec_batch, dec_seq, ATTN_WIDTH), k.reshape(dec_batch, dec_seq, ATTN_WIDTH),
                         v.reshape(dec_batch, dec_seq, ATTN_WIDTH), cache_kt, cache_vt, page_table, l)
        b_attn = o.reshape(dec_batch, N_HEADS, dec_seq, HEAD_DIM).transpose(0, 2, 1, 3).reshape(
            dec_rows, ATTN_WIDTH).astype(BF16)
        xs = _post(xs, a, b_attn, *post_w, tm=dec_rows)
        ks_rows.append(k.reshape(dec_batch, dec_seq, N_HEADS, HEAD_DIM))
        vs_rows.append(v.reshape(dec_batch, dec_seq, N_HEADS, HEAD_DIM))
        zs_rows.append(z.reshape(dec_batch, dec_seq, SGU_WIDTH))

    return (xp.reshape(batch, seq, d_model), xs.reshape(dec_batch, dec_seq, d_model),
            jnp.stack(kp_rows), jnp.stack(vp_rows), jnp.stack(ks_rows), jnp.stack(vs_rows), jnp.stack(zs_rows))
```

```python
import functools

import jax
import jax.numpy as jnp
from jax import lax
from jax.experimental import pallas as pl
from jax.experimental.pallas import tpu as pltpu

N_HEADS = 8
HEAD_DIM = 64
ATTN_WIDTH = N_HEADS * HEAD_DIM
SGU_GROUPS = 8
SGU_WIDTH = 512
SGU_GROUP_DIM = SGU_WIDTH // SGU_GROUPS
CHUNK = 128
MOBA_BLOCK = 256
MOBA_TOPK = 3
PAGE_SIZE = 128
EPS = 1e-6
NEG = -1e30
REAL_SCORE_FLOOR = -1e29
LOG2_E = 1.4426950408889634

LANES = 128
HALF = LANES // 2
N_PAIRS = N_HEADS // 2
VMEM_LIMIT_BYTES = 56 * 1024 * 1024

PROMPT_ROWS = 512
FFN_SPLIT = 4
SUM_ROWS = 16
Q_BLOCKS_PER_STEP = 2
KV_TILE = 1024
PAGES_PER_STEP = 32

F32 = jnp.float32
BF16 = jnp.bfloat16


def _rmsnorm(x, g):
    return x * lax.rsqrt(jnp.mean(x * x, axis=-1, keepdims=True) + EPS) * g


def _dot(a, b):
    return jnp.dot(a, b, preferred_element_type=F32)


def _dot_nt(a, b):
    return lax.dot_general(a, b, (((1,), (1,)), ((), ())), preferred_element_type=F32)


def _const_spec(shape):
    zeros = (0,) * len(shape)
    return pl.BlockSpec(shape, lambda *_: zeros, pipeline_mode=pl.Buffered(1))


def _row_spec(rows, cols):
    return pl.BlockSpec((rows, cols), lambda i: (i, 0))


def _inproj_kernel(x_ref, gpre_ref, w_ref, gsgu_ref, ws_ref, bias_ref, *out_refs, chunk, emit_attn):
    if emit_attn:
        q_ref, a_ref, kmean_ref, kaug_ref, kt_ref, vt_ref, vtb_ref = out_refs
    else:
        q_ref, a_ref, k_ref, v_ref, z_ref = out_refs
    tm = x_ref.shape[0]
    h = _rmsnorm(x_ref[...], gpre_ref[...]).astype(BF16)
    y = _dot(h, w_ref[...])
    q = y[:, 0:ATTN_WIDTH]
    k = y[:, ATTN_WIDTH:2 * ATTN_WIDTH]
    v = y[:, 2 * ATTN_WIDTH:3 * ATTN_WIDTH]
    u = y[:, 3 * ATTN_WIDTH:3 * ATTN_WIDTH + SGU_WIDTH]
    z = y[:, 3 * ATTN_WIDTH + SGU_WIDTH:]
    q_ref[...] = q.T if emit_attn else q
    zn = _rmsnorm(jax.nn.gelu(z), gsgu_ref[...])
    if not emit_attn:
        k_ref[...] = k
        v_ref[...] = v
        z_ref[...] = zn

    n_chunks = tm // chunk
    low = lax.broadcasted_iota(jnp.int32, (chunk, LANES), 1) < HALF
    w_row = lax.broadcasted_iota(jnp.int32, (chunk, 2 * chunk), 0)
    w_col = lax.broadcasted_iota(jnp.int32, (chunk, 2 * chunk), 1)
    causal = jnp.where(w_col >= chunk, w_col - chunk, w_col) <= w_row
    pair_out = []
    for p in range(N_PAIRS):
        w = jnp.where(causal, ws_ref[p], 0.0).astype(BF16)
        cols = []
        for c in range(n_chunks):
            zc = zn[c * chunk:(c + 1) * chunk, p * LANES:(p + 1) * LANES]
            cols.append(jnp.concatenate([jnp.where(low, zc, 0.0), jnp.where(low, 0.0, zc)], axis=0))
        rhs = jnp.concatenate(cols, axis=1).astype(BF16)
        pair_out.append(_dot(w, rhs))
    bias = bias_ref[...]
    rows = []
    for c in range(n_chunks):
        rows.append(jnp.concatenate([po[:, c * LANES:(c + 1) * LANES] for po in pair_out], axis=1) + bias)
    s = jnp.concatenate(rows, axis=0)
    a_ref[...] = (jax.nn.gelu(u) * s).astype(BF16)

    if emit_attn:
        kmean_ref[0] = jnp.mean(k.reshape(tm // MOBA_BLOCK, MOBA_BLOCK, ATTN_WIDTH), axis=1)
        row_g = pl.program_id(0) * tm + lax.broadcasted_iota(jnp.int32, (tm, LANES), 0)
        blk = jnp.right_shift(row_g, MOBA_BLOCK.bit_length() - 1)
        lane = lax.broadcasted_iota(jnp.int32, (tm, LANES), 1)
        low_t = lane < HALF
        onehot_hi = jnp.where(lane - HALF == blk, 1.0, 0.0)
        onehot_lo = jnp.where(lane == blk, 1.0, 0.0)
        for p in range(N_PAIRS):
            kc = k[:, p * LANES:(p + 1) * LANES]
            kaug_ref[:, (2 * p) * LANES:(2 * p + 1) * LANES] = jnp.where(low_t, kc, onehot_hi).astype(BF16)
            kaug_ref[:, (2 * p + 1) * LANES:(2 * p + 2) * LANES] = jnp.where(low_t, onehot_lo, kc).astype(BF16)
        kt_ref[...] = k.T
        vt = v.T
        vt_ref[...] = vt
        vtb_ref[0] = vt.astype(BF16)


def _inproj(x, g_pre, w_qkvuz, g_sgu, ws_pairs, bias, *, tm, chunk, emit_attn):
    rows, d_model = x.shape
    n_tiles = rows // tm
    out_shape = [jax.ShapeDtypeStruct((rows, ATTN_WIDTH), F32), jax.ShapeDtypeStruct((rows, SGU_WIDTH), BF16)]
    out_specs = [_row_spec(tm, ATTN_WIDTH), _row_spec(tm, SGU_WIDTH)]
    if emit_attn:
        out_shape[0] = jax.ShapeDtypeStruct((ATTN_WIDTH, rows), F32)
        out_specs[0] = pl.BlockSpec((ATTN_WIDTH, tm), lambda i: (0, i))
        blocks_per_tile = tm // MOBA_BLOCK
        out_shape += [
            jax.ShapeDtypeStruct((n_tiles, blocks_per_tile, ATTN_WIDTH), F32),
            jax.ShapeDtypeStruct((rows, N_HEADS * LANES), BF16),
            jax.ShapeDtypeStruct((ATTN_WIDTH, rows), F32),
            jax.ShapeDtypeStruct((ATTN_WIDTH, rows), F32),
            jax.ShapeDtypeStruct((rows // KV_TILE, ATTN_WIDTH, KV_TILE), BF16),
        ]
        tiles_per_kv = KV_TILE // tm
        out_specs += [
            pl.BlockSpec((1, blocks_per_tile, ATTN_WIDTH), lambda i: (i, 0, 0)),
            _row_spec(tm, N_HEADS * LANES),
            pl.BlockSpec((ATTN_WIDTH, tm), lambda i: (0, i)),
            pl.BlockSpec((ATTN_WIDTH, tm), lambda i: (0, i)),
            pl.BlockSpec((1, ATTN_WIDTH, tm), lambda i: (i // tiles_per_kv, 0, i % tiles_per_kv)),
        ]
    else:
        out_shape += [jax.ShapeDtypeStruct((rows, ATTN_WIDTH), F32)] * 2 + [
            jax.ShapeDtypeStruct((rows, SGU_WIDTH), F32)]
        out_specs += [_row_spec(tm, ATTN_WIDTH)] * 2 + [_row_spec(tm, SGU_WIDTH)]
    return pl.pallas_call(
        functools.partial(_inproj_kernel, chunk=chunk, emit_attn=emit_attn),
        out_shape=out_shape,
        grid=(n_tiles,),
        in_specs=[
            _row_spec(tm, d_model),
            _const_spec(g_pre.shape),
            _const_spec(w_qkvuz.shape),
            _const_spec(g_sgu.shape),
            _const_spec(ws_pairs.shape),
            _const_spec(bias.shape),
        ],
        out_specs=out_specs,
        compiler_params=pltpu.CompilerParams(
            dimension_semantics=("arbitrary",), vmem_limit_bytes=VMEM_LIMIT_BYTES),
        name="inproj_attn" if emit_attn else "inproj",
    )(x, g_pre, w_qkvuz, g_sgu, ws_pairs, bias)


def _select_kernel(qt_ref, ghi_ref, glo_ref, qaug_ref):
    i_f = pl.program_id(0).astype(F32)
    n_q = qt_ref.shape[1]
    scale = HEAD_DIM ** -0.5 * LOG2_E
    qt = qt_ref[...]
    q_hi = qt.astype(BF16)
    q_lo = (qt - q_hi.astype(F32)).astype(BF16)
    gates = _dot(ghi_ref[...], q_hi) + (_dot(ghi_ref[...], q_lo) + _dot(glo_ref[...], q_hi))
    block = lax.broadcasted_iota(jnp.int32, (HALF, n_q), 0).astype(F32)
    no_block = 2.0 * LANES
    for h in range(N_HEADS):
        rows = slice(h * HALF, (h + 1) * HALF)
        g = jnp.where(block < i_f, gates[rows], NEG)
        mask = jnp.full(g.shape, NEG, F32)
        for _ in range(MOBA_TOPK):
            m = jnp.max(g, axis=0, keepdims=True)
            first = jnp.min(jnp.where(g == m, block, no_block), axis=0, keepdims=True)
            pick = block == first
            mask = jnp.where(pick, jnp.where(g > REAL_SCORE_FLOOR, 0.0, mask), mask)
            g = jnp.where(pick, NEG, g)
        halves = [(qt[rows] * scale).astype(BF16), mask.astype(BF16)]
        if h % 2:
            halves.reverse()
        qaug_ref[h * LANES:(h + 1) * LANES, :] = jnp.concatenate(halves, axis=0)


def _select(qt, gmat):
    rows = qt.shape[1]
    g_hi = gmat.astype(BF16)
    g_lo = (gmat - g_hi.astype(F32)).astype(BF16)
    return pl.pallas_call(
        _select_kernel,
        out_shape=jax.ShapeDtypeStruct((N_HEADS * LANES, rows), BF16),
        grid=(rows // MOBA_BLOCK,),
        in_specs=[pl.BlockSpec((ATTN_WIDTH, MOBA_BLOCK), lambda i: (0, i)),
                  _const_spec(gmat.shape), _const_spec(gmat.shape)],
        out_specs=pl.BlockSpec((N_HEADS * LANES, MOBA_BLOCK), lambda i: (0, i)),
        compiler_params=pltpu.CompilerParams(
            dimension_semantics=("arbitrary",), vmem_limit_bytes=VMEM_LIMIT_BYTES),
        name="moba_select",
    )(qt, g_hi, g_lo)


def _moba_attn_kernel(qa_ref, ka_ref, vt_ref, vt_own_ref, o_ref):
    step = pl.program_id(1)
    blk = MOBA_BLOCK
    n_qb = qa_ref.shape[1] // blk
    tile = vt_ref.shape[2]
    blocks_per_tile = tile // blk
    streams = [(qb, p) for qb in range(n_qb) for p in range(2)]
    qs = [qa_ref[p * LANES:(p + 1) * LANES, qb * blk:(qb + 1) * blk] for qb, p in streams]

    def scores(k_rows, qs, which):
        return [_dot(k_rows[:, p * LANES:(p + 1) * LANES], qs[n]) if n in which else None
                for n, (_, p) in enumerate(streams)]

    ones_rows = jnp.ones((SUM_ROWS, blk), BF16)

    def absorb(state, s_list, vt_cols):
        new = []
        for n, (_, p) in enumerate(streams):
            m, acc = state[n]
            s = s_list[n]
            if s is not None:
                m_new = jnp.maximum(m, jnp.max(s, axis=0, keepdims=True))
                alpha = jnp.exp2(m - m_new)
                e = jnp.exp2((s - m_new).astype(BF16))
                vt_aug = jnp.concatenate([vt_cols[p * HALF:(p + 1) * HALF], ones_rows], axis=0)
                m, acc = m_new, alpha * acc + _dot(vt_aug, e)
            new.append((m, acc))
        return tuple(new)

    row = lax.broadcasted_iota(jnp.int32, (LANES, blk), 0)
    q_plain = [jnp.where(row < HALF if p == 0 else row >= HALF, q, jnp.zeros_like(q))
               for q, (_, p) in zip(qs, streams)]
    key_le_query = (lax.broadcasted_iota(jnp.int32, (blk, blk), 0)
                    <= lax.broadcasted_iota(jnp.int32, (blk, blk), 1))
    state = tuple((jnp.full((1, blk), NEG, F32), jnp.zeros((HALF + SUM_ROWS, blk), F32)) for _ in streams)
    for qb in range(n_qb):
        mine = [n for n, (b, _) in enumerate(streams) if b == qb]
        own = pl.multiple_of((step * n_qb + qb) * blk, blk)
        s_own = scores(ka_ref[pl.ds(own, blk), :], q_plain, mine)
        s_own = [None if s is None else jnp.where(key_le_query, s, NEG) for s in s_own]
        state = absorb(state, s_own, vt_own_ref[:, qb * blk:(qb + 1) * blk])

    everyone = range(len(streams))

    def past_tile(t, state):
        s_tile = scores(ka_ref[pl.ds(pl.multiple_of(t * tile, tile), tile), :], qs, everyone)
        for g in range(blocks_per_tile):
            rows = slice(g * blk, (g + 1) * blk)
            state = absorb(state, [s[rows] for s in s_tile], vt_ref[t, :, rows])
        return state

    latest_block = step * n_qb + n_qb - 1
    state = lax.fori_loop(0, (latest_block + blocks_per_tile - 1) // blocks_per_tile, past_tile, state)
    for qb in range(n_qb):
        out_t = jnp.concatenate([state[n][1][:HALF] / state[n][1][HALF:HALF + 1]
                                 for n, (b, _) in enumerate(streams) if b == qb], axis=0)
        o_ref[qb * blk:(qb + 1) * blk, :] = out_t.T.astype(BF16)


def _moba_attn(qaug_t, kaug, vtb):
    rows = kaug.shape[0]
    n_tiles = rows // KV_TILE
    q_rows = Q_BLOCKS_PER_STEP * MOBA_BLOCK
    steps_per_tile = KV_TILE // q_rows
    return pl.pallas_call(
        _moba_attn_kernel,
        out_shape=jax.ShapeDtypeStruct((rows, ATTN_WIDTH), BF16),
        grid=(N_PAIRS, rows // q_rows),
        in_specs=[
            pl.BlockSpec((2 * LANES, q_rows), lambda c, i: (c, i)),
            pl.BlockSpec((rows, 2 * LANES), lambda c, i: (0, c)),
            pl.BlockSpec((n_tiles, LANES, KV_TILE), lambda c, i: (0, c, 0)),
            pl.BlockSpec((None, LANES, q_rows), lambda c, i: (i // steps_per_tile, c, i % steps_per_tile)),
        ],
        out_specs=pl.BlockSpec((q_rows, LANES), lambda c, i: (i, c)),
        compiler_params=pltpu.CompilerParams(
            dimension_semantics=("arbitrary", "arbitrary"), vmem_limit_bytes=VMEM_LIMIT_BYTES),
        name="moba_prompt_attn",
    )(qaug_t, kaug, vtb, vtb)


def _gate_matrices(kmean):
    n_blocks = kmean.shape[0]
    km = kmean.reshape(n_blocks, N_HEADS, HEAD_DIM).transpose(1, 0, 2)
    km = jnp.pad(km, ((0, 0), (0, HALF - n_blocks), (0, 0)))
    same_head = jnp.eye(N_HEADS, dtype=F32)
    return (km[:, :, None, :] * same_head[:, None, :, None]).reshape(N_HEADS * HALF, ATTN_WIDTH)


def _page_index(b, s, pt_ref, *, layer, slot):
    return (layer, pt_ref[b, s * PAGES_PER_STEP + slot % PAGES_PER_STEP], 0, 0, 0)


def _head_diagonal(x, rows_per_head):
    return jnp.concatenate(
        [x[h * rows_per_head:(h + 1) * rows_per_head, (h // 2) * LANES:(h // 2 + 1) * LANES]
         for h in range(N_HEADS)], axis=0)


def _sample_attn_kernel(pt_ref, qbd_ref, knew_ref, vnew_ref, ownbias_ref, *rest):
    del pt_ref
    k_refs = rest[:PAGES_PER_STEP]
    v_refs = rest[PAGES_PER_STEP:2 * PAGES_PER_STEP]
    o_ref, m_ref, l_ref, g_ref, part_ref = rest[2 * PAGES_PER_STEP:]
    s = pl.program_id(1)
    n_pages = part_ref.shape[0]
    rows = qbd_ref.shape[1]
    tokens = rows // N_HEADS
    lane = lax.broadcasted_iota(jnp.int32, (rows, LANES), 1)
    lane_f = lane.astype(F32)
    qb = (qbd_ref[0] * HEAD_DIM ** -0.5).astype(BF16)

    @pl.when(s == 0)
    def _():
        m_ref[...] = jnp.full(m_ref.shape, NEG, F32)
        l_ref[...] = jnp.zeros(l_ref.shape, F32)
        g_ref[...] = jnp.zeros(g_ref.shape, F32)

    first_page = s * PAGES_PER_STEP
    scores = [_dot(qb, k_refs[slot][...].reshape(ATTN_WIDTH, PAGE_SIZE).astype(BF16))
              for slot in range(PAGES_PER_STEP)]
    m_all, l_all, g_all = m_ref[...], l_ref[...], g_ref[...]
    probs = []
    for slot, sc in enumerate(scores):
        m = jnp.max(sc, axis=-1, keepdims=True)
        p = jnp.exp(sc - m)
        here = lane == first_page + slot
        m_all = jnp.where(here, m, m_all)
        l_all = jnp.where(here, jnp.sum(p, axis=-1, keepdims=True), l_all)
        g_all = jnp.where(here, jnp.sum(sc, axis=-1, keepdims=True), g_all)
        probs.append(p.astype(BF16))
    m_ref[...], l_ref[...], g_ref[...] = m_all, l_all, g_all
    pair_rows = 2 * tokens
    for slot, pb in enumerate(probs):
        vt = v_refs[slot][...].reshape(ATTN_WIDTH, PAGE_SIZE).astype(BF16)
        part_ref[first_page + slot] = jnp.concatenate(
            [_dot_nt(pb[c * pair_rows:(c + 1) * pair_rows], vt[c * LANES:(c + 1) * LANES])
             for c in range(N_PAIRS)], axis=0)

    @pl.when(s == pl.num_programs(1) - 1)
    def _():
        gate = g_ref[...]
        even = jnp.bitwise_and(lane, 1) == 0
        gate = gate + jnp.where(even, pltpu.roll(gate, n_pages - 1, 1), pltpu.roll(gate, 1, 1))
        block_first_lane = (lane - jnp.bitwise_and(lane, 1)).astype(F32)
        keep = jnp.zeros(gate.shape, jnp.bool_)
        for _ in range(MOBA_TOPK):
            gm = jnp.max(gate, axis=-1, keepdims=True)
            first = jnp.min(jnp.where(gate == gm, lane_f, 2.0 * LANES), axis=-1, keepdims=True)
            pick = block_first_lane == first
            keep = keep | pick
            gate = jnp.where(pick, NEG, gate)
        m_all = m_ref[...]
        s_own = _dot_nt(qb, knew_ref[0].astype(BF16)) + ownbias_ref[...]
        m_fin = jnp.maximum(jnp.max(jnp.where(keep, m_all, NEG), axis=-1, keepdims=True),
                            jnp.max(s_own, axis=-1, keepdims=True))
        p_own = jnp.exp(s_own - m_fin)
        w = jnp.where(keep, jnp.exp(m_all - m_fin), 0.0)
        l_fin = jnp.sum(p_own, axis=-1, keepdims=True) + jnp.sum(w * l_ref[...], axis=-1, keepdims=True)
        acc = _head_diagonal(_dot(p_own.astype(BF16), vnew_ref[0].astype(BF16)), tokens)
        for page in range(n_pages):
            acc = acc + w[:, page:page + 1] * part_ref[page]
        acc = acc / l_fin
        row = lax.broadcasted_iota(jnp.int32, (rows, LANES), 0)
        odd_head = jnp.bitwise_and(row, tokens) == tokens
        o_ref[0] = jnp.where(odd_head, pltpu.roll(acc, HALF, 1), acc)[:, :HEAD_DIM]


def _sample_attn(q, k_new, v_new, cache_kt, cache_vt, page_table, layer):
    n_seq, tokens, _ = q.shape
    n_pages = page_table.shape[1]
    rows = N_HEADS * tokens
    assert n_pages == LANES and MOBA_BLOCK == 2 * PAGE_SIZE
    assert tokens <= LANES and tokens & (tokens - 1) == 0
    assert n_pages * PAGE_SIZE // MOBA_BLOCK >= MOBA_TOPK
    q4 = q.reshape(n_seq, tokens, N_HEADS, HEAD_DIM).transpose(0, 2, 1, 3)
    qbd = (q4[:, :, :, None, :] * jnp.eye(N_HEADS, dtype=F32)[None, :, None, :, None]).reshape(
        n_seq, rows, ATTN_WIDTH)
    pad = ((0, 0), (0, LANES - tokens), (0, 0))
    k_new = jnp.pad(k_new, pad)
    v_new = jnp.pad(v_new, pad)
    t_q = jnp.arange(rows)[:, None] % tokens
    ownbias = jnp.where(jnp.arange(LANES)[None, :] <= t_q, 0.0, NEG).astype(F32)
    page_block = (None, None, N_HEADS, HEAD_DIM, PAGE_SIZE)
    page_specs = [pl.BlockSpec(page_block, functools.partial(_page_index, layer=layer, slot=slot))
                  for slot in range(2 * PAGES_PER_STEP)]
    grid_spec = pltpu.PrefetchScalarGridSpec(
        num_scalar_prefetch=1,
        grid=(n_seq, n_pages // PAGES_PER_STEP),
        in_specs=[
            pl.BlockSpec((1, rows, ATTN_WIDTH), lambda b, s, pt: (b, 0, 0)),
            pl.BlockSpec((1, LANES, ATTN_WIDTH), lambda b, s, pt: (b, 0, 0)),
            pl.BlockSpec((1, LANES, ATTN_WIDTH), lambda b, s, pt: (b, 0, 0)),
            pl.BlockSpec(ownbias.shape, lambda b, s, pt: (0, 0)),
        ] + page_specs,
        out_specs=pl.BlockSpec((1, rows, HEAD_DIM), lambda b, s, pt: (b, 0, 0)),
        scratch_shapes=[
            pltpu.VMEM((rows, n_pages), F32),
            pltpu.VMEM((rows, n_pages), F32),
            pltpu.VMEM((rows, n_pages), F32),
            pltpu.VMEM((n_pages, rows, LANES), F32),
        ],
    )
    return pl.pallas_call(
        _sample_attn_kernel,
        out_shape=jax.ShapeDtypeStruct((n_seq, rows, HEAD_DIM), F32),
        grid_spec=grid_spec,
        compiler_params=pltpu.CompilerParams(
            dimension_semantics=("arbitrary", "arbitrary"), vmem_limit_bytes=VMEM_LIMIT_BYTES),
        name="moba_sample_attn",
    )(page_table, qbd, k_new, v_new, ownbias,
      *([cache_kt] * PAGES_PER_STEP), *([cache_vt] * PAGES_PER_STEP))


def _post_kernel(x_ref, a_ref, b_ref, gpre_ref, wg_ref, wpa_ref, wpb_ref, wout_ref,
                 gpm_ref, gpf_ref, wup_ref, wdown_ref, gff_ref, o_ref):
    x = x_ref[...]
    d_model = x.shape[1]
    h = _rmsnorm(x, gpre_ref[...]).astype(BF16)
    gates = jax.nn.sigmoid(_dot(h, wg_ref[...]))
    merged = (gates[:, :d_model] * _dot(a_ref[...], wpa_ref[...])
              + gates[:, d_model:] * _dot(b_ref[...], wpb_ref[...]))
    x = x + _rmsnorm(_dot(merged.astype(BF16), wout_ref[...]), gpm_ref[...])
    h2 = _rmsnorm(x, gpf_ref[...]).astype(BF16)
    slab = wup_ref.shape[1] // FFN_SPLIT
    f = jnp.zeros(x.shape, F32)
    for c in range(FFN_SPLIT):
        up = _dot(h2, wup_ref[:, c * slab:(c + 1) * slab])
        f = f + _dot(jnp.square(jnp.maximum(up, 0.0)).astype(BF16), wdown_ref[c * slab:(c + 1) * slab, :])
    o_ref[...] = x + _rmsnorm(f, gff_ref[...])


def _post(x, a, b, g_pre, w_g, w_pa, w_pb, w_out, g_pm, g_pf, w_up, w_down, g_ff, *, tm):
    rows, d_model = x.shape
    consts = (g_pre, w_g, w_pa, w_pb, w_out, g_pm, g_pf, w_up, w_down, g_ff)
    return pl.pallas_call(
        _post_kernel,
        out_shape=jax.ShapeDtypeStruct((rows, d_model), F32),
        grid=(rows // tm,),
        in_specs=[_row_spec(tm, d_model), _row_spec(tm, SGU_WIDTH), _row_spec(tm, ATTN_WIDTH)]
        + [_const_spec(c.shape) for c in consts],
        out_specs=_row_spec(tm, d_model),
        compiler_params=pltpu.CompilerParams(
            dimension_semantics=("arbitrary",), vmem_limit_bytes=VMEM_LIMIT_BYTES),
        name="merge_mlp",
    )(x, a, b, *consts)


def _pair_weights(w):
    g, p, j = w.shape
    return w.reshape(g // 2, 2, p, j).transpose(0, 2, 1, 3).reshape(g // 2, p, 2 * j)


def kernel(x_prompt, x_sample, cache_k, cache_v, page_table, g_pre_mix, w_in, g_sgu, w_s, b_s, w_pa, w_pb,
           w_out, g_post_mix, g_pre_ffn, w_up, w_down, g_post_ffn):
    depth = w_in.shape[0]
    batch, seq, d_model = x_prompt.shape
    dec_batch, dec_seq, _ = x_sample.shape
    assert batch == 1 and seq % KV_TILE == 0 and KV_TILE % PROMPT_ROWS == 0 and seq // MOBA_BLOCK <= HALF
    assert dec_seq <= CHUNK and SGU_GROUP_DIM == HALF and HEAD_DIM == HALF
    n_qkvuz = 3 * ATTN_WIDTH + 2 * SGU_WIDTH
    dec_rows = dec_batch * dec_seq

    cache_kt = cache_k.transpose(0, 1, 3, 4, 2)
    cache_vt = cache_v.transpose(0, 1, 3, 4, 2)
    xp = x_prompt.reshape(seq, d_model)
    xs = x_sample.reshape(dec_rows, d_model)
    kp_rows, vp_rows, ks_rows, vs_rows, zs_rows = [], [], [], [], []
    for l in range(depth):
        row = lambda g: g[l][None, :]
        w_qkvuz = w_in[l][:, :n_qkvuz].astype(BF16)
        w_gates = w_in[l][:, n_qkvuz:].astype(BF16)
        post_w = (row(g_pre_mix), w_gates, w_pa[l].astype(BF16), w_pb[l].astype(BF16), w_out[l].astype(BF16),
                  row(g_post_mix), row(g_pre_ffn), w_up[l].astype(BF16), w_down[l].astype(BF16), row(g_post_ffn))

        bias_p = jnp.repeat(b_s[l].T, SGU_GROUP_DIM, axis=1)
        qt, a, kmean, kaug, kt, vt, vtb = _inproj(
            xp, row(g_pre_mix), w_qkvuz, row(g_sgu), _pair_weights(w_s[l]), bias_p,
            tm=PROMPT_ROWS, chunk=CHUNK, emit_attn=True)
        qaug_t = _select(qt, _gate_matrices(kmean.reshape(seq // MOBA_BLOCK, ATTN_WIDTH)))
        b_attn = _moba_attn(qaug_t, kaug, vtb)
        xp = _post(xp, a, b_attn, *post_w, tm=PROMPT_ROWS)
        kp_rows.append(kt.reshape(N_HEADS, HEAD_DIM, seq).transpose(2, 0, 1)[None])
        vp_rows.append(vt.reshape(N_HEADS, HEAD_DIM, seq).transpose(2, 0, 1)[None])

        pos = jnp.arange(dec_rows) % dec_seq
        seq_id = jnp.arange(dec_rows) // dec_seq
        spread = jax.nn.one_hot(pos, dec_seq, dtype=F32)
        w_tiled = jnp.einsum("rp,gpj,cj->grc", spread, w_s[l][:, :dec_seq, :dec_seq], spread,
                             precision=lax.Precision.HIGHEST)
        w_blockdiag = jnp.where(seq_id[:, None] == seq_id[None, :], w_tiled, 0.0)
        bias_s = jnp.tile(bias_p[:dec_seq], (dec_batch, 1))
        q, a, k, v, z = _inproj(
            xs, row(g_pre_mix), w_qkvuz, row(g_sgu), _pair_weights(w_blockdiag), bias_s,
            tm=dec_rows, chunk=dec_rows, emit_attn=False)
        o = _sample_attn(q.reshape(dec_batch, dec_seq, ATTN_WIDTH), k.reshape(dec_batch, dec_seq, ATTN_WIDTH),
                         v.reshape(dec_batch, dec_seq, ATTN_WIDTH), cache_kt, cache_vt, page_table, l)
        b_attn = o.reshape(dec_batch, N_HEADS, dec_seq, HEAD_DIM).transpose(0, 2, 1, 3).reshape(
            dec_rows, ATTN_WIDTH).astype(BF16)
        xs = _post(xs, a, b_attn, *post_w, tm=dec_rows)
        ks_rows.append(k.reshape(dec_batch, dec_seq, N_HEADS, HEAD_DIM))
        vs_rows.append(v.reshape(dec_batch, dec_seq, N_HEADS, HEAD_DIM))
        zs_rows.append(z.reshape(dec_batch, dec_seq, SGU_WIDTH))

    return (xp.reshape(batch, seq, d_model), xs.reshape(dec_batch, dec_seq, d_model),
            jnp.stack(kp_rows), jnp.stack(vp_rows), jnp.stack(ks_rows), jnp.stack(vs_rows), jnp.stack(zs_rows))
```

```python
import functools

import jax
import jax.numpy as jnp
from jax import lax
from jax.experimental import pallas as pl
from jax.experimental.pallas import tpu as pltpu

N_HEADS = 8
HEAD_DIM = 64
ATTN_WIDTH = N_HEADS * HEAD_DIM
SGU_GROUPS = 8
SGU_WIDTH = 512
SGU_GROUP_DIM = SGU_WIDTH // SGU_GROUPS
CHUNK = 128
MOBA_BLOCK = 256
MOBA_TOPK = 3
PAGE_SIZE = 128
EPS = 1e-6
NEG = -1e30
REAL_SCORE_FLOOR = -1e29
LOG2_E = 1.4426950408889634

LANES = 128
HALF = LANES // 2
N_PAIRS = N_HEADS // 2
VMEM_LIMIT_BYTES = 56 * 1024 * 1024

PROMPT_ROWS = 512
FFN_SPLIT = 4
SUM_ROWS = 16
Q_BLOCKS_PER_STEP = 4
KV_TILE = 1024
PAGES_PER_STEP = 32

F32 = jnp.float32
BF16 = jnp.bfloat16


def _rmsnorm(x, g):
    return x * lax.rsqrt(jnp.mean(x * x, axis=-1, keepdims=True) + EPS) * g


def _dot(a, b):
    return jnp.dot(a, b, preferred_element_type=F32)


def _dot_nt(a, b):
    return lax.dot_general(a, b, (((1,), (1,)), ((), ())), preferred_element_type=F32)


def _const_spec(shape):
    zeros = (0,) * len(shape)
    return pl.BlockSpec(shape, lambda *_: zeros, pipeline_mode=pl.Buffered(1))


def _row_spec(rows, cols):
    return pl.BlockSpec((rows, cols), lambda i: (i, 0))


def _inproj_kernel(x_ref, gpre_ref, w_ref, gsgu_ref, ws_ref, bias_ref, *out_refs, chunk, emit_attn):
    if emit_attn:
        q_ref, a_ref, kmean_ref, kaug_ref, kt_ref, vt_ref, vtb_ref = out_refs
    else:
        q_ref, a_ref, k_ref, v_ref, z_ref = out_refs
    tm = x_ref.shape[0]
    h = _rmsnorm(x_ref[...], gpre_ref[...]).astype(BF16)
    y = _dot(h, w_ref[...])
    q = y[:, 0:ATTN_WIDTH]
    k = y[:, ATTN_WIDTH:2 * ATTN_WIDTH]
    v = y[:, 2 * ATTN_WIDTH:3 * ATTN_WIDTH]
    u = y[:, 3 * ATTN_WIDTH:3 * ATTN_WIDTH + SGU_WIDTH]
    z = y[:, 3 * ATTN_WIDTH + SGU_WIDTH:]
    q_ref[...] = q.T if emit_attn else q
    zn = _rmsnorm(jax.nn.gelu(z), gsgu_ref[...])
    if not emit_attn:
        k_ref[...] = k
        v_ref[...] = v
        z_ref[...] = zn

    n_chunks = tm // chunk
    low = lax.broadcasted_iota(jnp.int32, (chunk, LANES), 1) < HALF
    w_row = lax.broadcasted_iota(jnp.int32, (chunk, 2 * chunk), 0)
    w_col = lax.broadcasted_iota(jnp.int32, (chunk, 2 * chunk), 1)
    causal = jnp.where(w_col >= chunk, w_col - chunk, w_col) <= w_row
    pair_out = []
    for p in range(N_PAIRS):
        w = jnp.where(causal, ws_ref[p], 0.0).astype(BF16)
        cols = []
        for c in range(n_chunks):
            zc = zn[c * chunk:(c + 1) * chunk, p * LANES:(p + 1) * LANES]
            cols.append(jnp.concatenate([jnp.where(low, zc, 0.0), jnp.where(low, 0.0, zc)], axis=0))
        rhs = jnp.concatenate(cols, axis=1).astype(BF16)
        pair_out.append(_dot(w, rhs))
    bias = bias_ref[...]
    rows = []
    for c in range(n_chunks):
        rows.append(jnp.concatenate([po[:, c * LANES:(c + 1) * LANES] for po in pair_out], axis=1) + bias)
    s = jnp.concatenate(rows, axis=0)
    a_ref[...] = (jax.nn.gelu(u) * s).astype(BF16)

    if emit_attn:
        kmean_ref[0] = jnp.mean(k.reshape(tm // MOBA_BLOCK, MOBA_BLOCK, ATTN_WIDTH), axis=1)
        row_g = pl.program_id(0) * tm + lax.broadcasted_iota(jnp.int32, (tm, LANES), 0)
        blk = jnp.right_shift(row_g, MOBA_BLOCK.bit_length() - 1)
        lane = lax.broadcasted_iota(jnp.int32, (tm, LANES), 1)
        low_t = lane < HALF
        onehot_hi = jnp.where(lane - HALF == blk, 1.0, 0.0)
        onehot_lo = jnp.where(lane == blk, 1.0, 0.0)
        for p in range(N_PAIRS):
            kc = k[:, p * LANES:(p + 1) * LANES]
            kaug_ref[:, (2 * p) * LANES:(2 * p + 1) * LANES] = jnp.where(low_t, kc, onehot_hi).astype(BF16)
            kaug_ref[:, (2 * p + 1) * LANES:(2 * p + 2) * LANES] = jnp.where(low_t, onehot_lo, kc).astype(BF16)
        kt_ref[...] = k.T
        vt = v.T
        vt_ref[...] = vt
        vtb_ref[0] = vt.astype(BF16)


def _inproj(x, g_pre, w_qkvuz, g_sgu, ws_pairs, bias, *, tm, chunk, emit_attn):
    rows, d_model = x.shape
    n_tiles = rows // tm
    out_shape = [jax.ShapeDtypeStruct((rows, ATTN_WIDTH), F32), jax.ShapeDtypeStruct((rows, SGU_WIDTH), BF16)]
    out_specs = [_row_spec(tm, ATTN_WIDTH), _row_spec(tm, SGU_WIDTH)]
    if emit_attn:
        out_shape[0] = jax.ShapeDtypeStruct((ATTN_WIDTH, rows), F32)
        out_specs[0] = pl.BlockSpec((ATTN_WIDTH, tm), lambda i: (0, i))
        blocks_per_tile = tm // MOBA_BLOCK
        out_shape += [
            jax.ShapeDtypeStruct((n_tiles, blocks_per_tile, ATTN_WIDTH), F32),
            jax.ShapeDtypeStruct((rows, N_HEADS * LANES), BF16),
            jax.ShapeDtypeStruct((ATTN_WIDTH, rows), F32),
            jax.ShapeDtypeStruct((ATTN_WIDTH, rows), F32),
            jax.ShapeDtypeStruct((rows // KV_TILE, ATTN_WIDTH, KV_TILE), BF16),
        ]
        tiles_per_kv = KV_TILE // tm
        out_specs += [
            pl.BlockSpec((1, blocks_per_tile, ATTN_WIDTH), lambda i: (i, 0, 0)),
            _row_spec(tm, N_HEADS * LANES),
            pl.BlockSpec((ATTN_WIDTH, tm), lambda i: (0, i)),
            pl.BlockSpec((ATTN_WIDTH, tm), lambda i: (0, i)),
            pl.BlockSpec((1, ATTN_WIDTH, tm), lambda i: (i // tiles_per_kv, 0, i % tiles_per_kv)),
        ]
    else:
        out_shape += [jax.ShapeDtypeStruct((rows, ATTN_WIDTH), F32)] * 2 + [
            jax.ShapeDtypeStruct((rows, SGU_WIDTH), F32)]
        out_specs += [_row_spec(tm, ATTN_WIDTH)] * 2 + [_row_spec(tm, SGU_WIDTH)]
    return pl.pallas_call(
        functools.partial(_inproj_kernel, chunk=chunk, emit_attn=emit_attn),
        out_shape=out_shape,
        grid=(n_tiles,),
        in_specs=[
            _row_spec(tm, d_model),
            _const_spec(g_pre.shape),
            _const_spec(w_qkvuz.shape),
            _const_spec(g_sgu.shape),
            _const_spec(ws_pairs.shape),
            _const_spec(bias.shape),
        ],
        out_specs=out_specs,
        compiler_params=pltpu.CompilerParams(
            dimension_semantics=("arbitrary",), vmem_limit_bytes=VMEM_LIMIT_BYTES),
        name="inproj_attn" if emit_attn else "inproj",
    )(x, g_pre, w_qkvuz, g_sgu, ws_pairs, bias)


def _select_kernel(qt_ref, ghi_ref, glo_ref, qaug_ref):
    i_f = pl.program_id(0).astype(F32)
    n_q = qt_ref.shape[1]
    scale = HEAD_DIM ** -0.5 * LOG2_E
    qt = qt_ref[...]
    q_hi = qt.astype(BF16)
    q_lo = (qt - q_hi.astype(F32)).astype(BF16)
    gates = _dot(ghi_ref[...], q_hi) + (_dot(ghi_ref[...], q_lo) + _dot(glo_ref[...], q_hi))
    block = lax.broadcasted_iota(jnp.int32, (HALF, n_q), 0).astype(F32)
    no_block = 2.0 * LANES
    for h in range(N_HEADS):
        rows = slice(h * HALF, (h + 1) * HALF)
        g = jnp.where(block < i_f, gates[rows], NEG)
        mask = jnp.full(g.shape, NEG, F32)
        for _ in range(MOBA_TOPK):
            m = jnp.max(g, axis=0, keepdims=True)
            first = jnp.min(jnp.where(g == m, block, no_block), axis=0, keepdims=True)
            pick = block == first
            mask = jnp.where(pick, jnp.where(g > REAL_SCORE_FLOOR, 0.0, mask), mask)
            g = jnp.where(pick, NEG, g)
        halves = [(qt[rows] * scale).astype(BF16), mask.astype(BF16)]
        if h % 2:
            halves.reverse()
        qaug_ref[h * LANES:(h + 1) * LANES, :] = jnp.concatenate(halves, axis=0)


def _select(qt, gmat):
    rows = qt.shape[1]
    g_hi = gmat.astype(BF16)
    g_lo = (gmat - g_hi.astype(F32)).astype(BF16)
    return pl.pallas_call(
        _select_kernel,
        out_shape=jax.ShapeDtypeStruct((N_HEADS * LANES, rows), BF16),
        grid=(rows // MOBA_BLOCK,),
        in_specs=[pl.BlockSpec((ATTN_WIDTH, MOBA_BLOCK), lambda i: (0, i)),
                  _const_spec(gmat.shape), _const_spec(gmat.shape)],
        out_specs=pl.BlockSpec((N_HEADS * LANES, MOBA_BLOCK), lambda i: (0, i)),
        compiler_params=pltpu.CompilerParams(
            dimension_semantics=("arbitrary",), vmem_limit_bytes=VMEM_LIMIT_BYTES),
        name="moba_select",
    )(qt, g_hi, g_lo)


def _moba_attn_kernel(qa_ref, ka_ref, vt_ref, vt_own_ref, o_ref):
    step = pl.program_id(1)
    blk = MOBA_BLOCK
    n_qb = qa_ref.shape[1] // blk
    tile = vt_ref.shape[2]
    blocks_per_tile = tile // blk
    streams = [(qb, p) for qb in range(n_qb) for p in range(2)]
    qs = [qa_ref[p * LANES:(p + 1) * LANES, qb * blk:(qb + 1) * blk] for qb, p in streams]

    def scores(k_rows, qs, which):
        return [_dot(k_rows[:, p * LANES:(p + 1) * LANES], qs[n]) if n in which else None
                for n, (_, p) in enumerate(streams)]

    ones_rows = jnp.ones((SUM_ROWS, blk), BF16)

    def absorb(state, s_list, vt_cols):
        new = []
        for n, (_, p) in enumerate(streams):
            m, acc = state[n]
            s = s_list[n]
            if s is not None:
                m_new = jnp.maximum(m, jnp.max(s, axis=0, keepdims=True))
                alpha = jnp.exp2(m - m_new)
                e = jnp.exp2((s - m_new).astype(BF16))
                vt_aug = jnp.concatenate([vt_cols[p * HALF:(p + 1) * HALF], ones_rows], axis=0)
                m, acc = m_new, alpha * acc + _dot(vt_aug, e)
            new.append((m, acc))
        return tuple(new)

    row = lax.broadcasted_iota(jnp.int32, (LANES, blk), 0)
    q_plain = [jnp.where(row < HALF if p == 0 else row >= HALF, q, jnp.zeros_like(q))
               for q, (_, p) in zip(qs, streams)]
    key_le_query = (lax.broadcasted_iota(jnp.int32, (blk, blk), 0)
                    <= lax.broadcasted_iota(jnp.int32, (blk, blk), 1))
    state = tuple((jnp.full((1, blk), NEG, F32), jnp.zeros((HALF + SUM_ROWS, blk), F32)) for _ in streams)
    for qb in range(n_qb):
        mine = [n for n, (b, _) in enumerate(streams) if b == qb]
        own = pl.multiple_of((step * n_qb + qb) * blk, blk)
        s_own = scores(ka_ref[pl.ds(own, blk), :], q_plain, mine)
        s_own = [None if s is None else jnp.where(key_le_query, s, NEG) for s in s_own]
        state = absorb(state, s_own, vt_own_ref[:, qb * blk:(qb + 1) * blk])

    everyone = range(len(streams))

    def past_tile(t, state):
        s_tile = scores(ka_ref[pl.ds(pl.multiple_of(t * tile, tile), tile), :], qs, everyone)
        for g in range(blocks_per_tile):
            rows = slice(g * blk, (g + 1) * blk)
            state = absorb(state, [s[rows] for s in s_tile], vt_ref[t, :, rows])
        return state

    latest_block = step * n_qb + n_qb - 1
    state = lax.fori_loop(0, (latest_block + blocks_per_tile - 1) // blocks_per_tile, past_tile, state)
    for qb in range(n_qb):
        out_t = jnp.concatenate([state[n][1][:HALF] / state[n][1][HALF:HALF + 1]
                                 for n, (b, _) in enumerate(streams) if b == qb], axis=0)
        o_ref[qb * blk:(qb + 1) * blk, :] = out_t.T.astype(BF16)


def _moba_attn(qaug_t, kaug, vtb):
    rows = kaug.shape[0]
    n_tiles = rows // KV_TILE
    q_rows = Q_BLOCKS_PER_STEP * MOBA_BLOCK
    steps_per_tile = KV_TILE // q_rows
    return pl.pallas_call(
        _moba_attn_kernel,
        out_shape=jax.ShapeDtypeStruct((rows, ATTN_WIDTH), BF16),
        grid=(N_PAIRS, rows // q_rows),
        in_specs=[
            pl.BlockSpec((2 * LANES, q_rows), lambda c, i: (c, i)),
            pl.BlockSpec((rows, 2 * LANES), lambda c, i: (0, c)),
            pl.BlockSpec((n_tiles, LANES, KV_TILE), lambda c, i: (0, c, 0)),
            pl.BlockSpec((None, LANES, q_rows), lambda c, i: (i // steps_per_tile, c, i % steps_per_tile)),
        ],
        out_specs=pl.BlockSpec((q_rows, LANES), lambda c, i: (i, c)),
        compiler_params=pltpu.CompilerParams(
            dimension_semantics=("arbitrary", "arbitrary"), vmem_limit_bytes=VMEM_LIMIT_BYTES),
        name="moba_prompt_attn",
    )(qaug_t, kaug, vtb, vtb)


def _gate_matrices(kmean):
    n_blocks = kmean.shape[0]
    km = kmean.reshape(n_blocks, N_HEADS, HEAD_DIM).transpose(1, 0, 2)
    km = jnp.pad(km, ((0, 0), (0, HALF - n_blocks), (0, 0)))
    same_head = jnp.eye(N_HEADS, dtype=F32)
    return (km[:, :, None, :] * same_head[:, None, :, None]).reshape(N_HEADS * HALF, ATTN_WIDTH)


def _page_index(b, s, pt_ref, *, layer, slot):
    return (layer, pt_ref[b, s * PAGES_PER_STEP + slot % PAGES_PER_STEP], 0, 0, 0)


def _head_diagonal(x, rows_per_head):
    return jnp.concatenate(
        [x[h * rows_per_head:(h + 1) * rows_per_head, (h // 2) * LANES:(h // 2 + 1) * LANES]
         for h in range(N_HEADS)], axis=0)


def _sample_attn_kernel(pt_ref, qbd_ref, knew_ref, vnew_ref, ownbias_ref, *rest):
    del pt_ref
    k_refs = rest[:PAGES_PER_STEP]
    v_refs = rest[PAGES_PER_STEP:2 * PAGES_PER_STEP]
    o_ref, m_ref, l_ref, g_ref, part_ref = rest[2 * PAGES_PER_STEP:]
    s = pl.program_id(1)
    n_pages = part_ref.shape[0]
    rows = qbd_ref.shape[1]
    tokens = rows // N_HEADS
    lane = lax.broadcasted_iota(jnp.int32, (rows, LANES), 1)
    lane_f = lane.astype(F32)
    qb = (qbd_ref[0] * HEAD_DIM ** -0.5).astype(BF16)

    @pl.when(s == 0)
    def _():
        m_ref[...] = jnp.full(m_ref.shape, NEG, F32)
        l_ref[...] = jnp.zeros(l_ref.shape, F32)
        g_ref[...] = jnp.zeros(g_ref.shape, F32)

    first_page = s * PAGES_PER_STEP
    scores = [_dot(qb, k_refs[slot][...].reshape(ATTN_WIDTH, PAGE_SIZE).astype(BF16))
              for slot in range(PAGES_PER_STEP)]
    m_all, l_all, g_all = m_ref[...], l_ref[...], g_ref[...]
    probs = []
    for slot, sc in enumerate(scores):
        m = jnp.max(sc, axis=-1, keepdims=True)
        p = jnp.exp(sc - m)
        here = lane == first_page + slot
        m_all = jnp.where(here, m, m_all)
        l_all = jnp.where(here, jnp.sum(p, axis=-1, keepdims=True), l_all)
        g_all = jnp.where(here, jnp.sum(sc, axis=-1, keepdims=True), g_all)
        probs.append(p.astype(BF16))
    m_ref[...], l_ref[...], g_ref[...] = m_all, l_all, g_all
    pair_rows = 2 * tokens
    for slot, pb in enumerate(probs):
        vt = v_refs[slot][...].reshape(ATTN_WIDTH, PAGE_SIZE).astype(BF16)
        part_ref[first_page + slot] = jnp.concatenate(
            [_dot_nt(pb[c * pair_rows:(c + 1) * pair_rows], vt[c * LANES:(c + 1) * LANES])
             for c in range(N_PAIRS)], axis=0)

    @pl.when(s == pl.num_programs(1) - 1)
    def _():
        gate = g_ref[...]
        even = jnp.bitwise_and(lane, 1) == 0
        gate = gate + jnp.where(even, pltpu.roll(gate, n_pages - 1, 1), pltpu.roll(gate, 1, 1))
        block_first_lane = (lane - jnp.bitwise_and(lane, 1)).astype(F32)
        keep = jnp.zeros(gate.shape, jnp.bool_)
        for _ in range(MOBA_TOPK):
            gm = jnp.max(gate, axis=-1, keepdims=True)
            first = jnp.min(jnp.where(gate == gm, lane_f, 2.0 * LANES), axis=-1, keepdims=True)
            pick = block_first_lane == first
            keep = keep | pick
            gate = jnp.where(pick, NEG, gate)
        m_all = m_ref[...]
        s_own = _dot_nt(qb, knew_ref[0].astype(BF16)) + ownbias_ref[...]
        m_fin = jnp.maximum(jnp.max(jnp.where(keep, m_all, NEG), axis=-1, keepdims=True),
                            jnp.max(s_own, axis=-1, keepdims=True))
        p_own = jnp.exp(s_own - m_fin)
        w = jnp.where(keep, jnp.exp(m_all - m_fin), 0.0)
        l_fin = jnp.sum(p_own, axis=-1, keepdims=True) + jnp.sum(w * l_ref[...], axis=-1, keepdims=True)
        acc = _head_diagonal(_dot(p_own.astype(BF16), vnew_ref[0].astype(BF16)), tokens)
        for page in range(n_pages):
            acc = acc + w[:, page:page + 1] * part_ref[page]
        acc = acc / l_fin
        row = lax.broadcasted_iota(jnp.int32, (rows, LANES), 0)
        odd_head = jnp.bitwise_and(row, tokens) == tokens
        o_ref[0] = jnp.where(odd_head, pltpu.roll(acc, HALF, 1), acc)[:, :HEAD_DIM]


def _sample_attn(q, k_new, v_new, cache_kt, cache_vt, page_table, layer):
    n_seq, tokens, _ = q.shape
    n_pages = page_table.shape[1]
    rows = N_HEADS * tokens
    assert n_pages == LANES and MOBA_BLOCK == 2 * PAGE_SIZE
    assert tokens <= LANES and tokens & (tokens - 1) == 0
    assert n_pages * PAGE_SIZE // MOBA_BLOCK >= MOBA_TOPK
    q4 = q.reshape(n_seq, tokens, N_HEADS, HEAD_DIM).transpose(0, 2, 1, 3)
    qbd = (q4[:, :, :, None, :] * jnp.eye(N_HEADS, dtype=F32)[None, :, None, :, None]).reshape(
        n_seq, rows, ATTN_WIDTH)
    pad = ((0, 0), (0, LANES - tokens), (0, 0))
    k_new = jnp.pad(k_new, pad)
    v_new = jnp.pad(v_new, pad)
    t_q = jnp.arange(rows)[:, None] % tokens
    ownbias = jnp.where(jnp.arange(LANES)[None, :] <= t_q, 0.0, NEG).astype(F32)
    page_block = (None, None, N_HEADS, HEAD_DIM, PAGE_SIZE)
    page_specs = [pl.BlockSpec(page_block, functools.partial(_page_index, layer=layer, slot=slot))
                  for slot in range(2 * PAGES_PER_STEP)]
    grid_spec = pltpu.PrefetchScalarGridSpec(
        num_scalar_prefetch=1,
        grid=(n_seq, n_pages // PAGES_PER_STEP),
        in_specs=[
            pl.BlockSpec((1, rows, ATTN_WIDTH), lambda b, s, pt: (b, 0, 0)),
            pl.BlockSpec((1, LANES, ATTN_WIDTH), lambda b, s, pt: (b, 0, 0)),
            pl.BlockSpec((1, LANES, ATTN_WIDTH), lambda b, s, pt: (b, 0, 0)),
            pl.BlockSpec(ownbias.shape, lambda b, s, pt: (0, 0)),
        ] + page_specs,
        out_specs=pl.BlockSpec((1, rows, HEAD_DIM), lambda b, s, pt: (b, 0, 0)),
        scratch_shapes=[
            pltpu.VMEM((rows, n_pages), F32),
            pltpu.VMEM((rows, n_pages), F32),
            pltpu.VMEM((rows, n_pages), F32),
            pltpu.VMEM((n_pages, rows, LANES), F32),
        ],
    )
    return pl.pallas_call(
        _sample_attn_kernel,
        out_shape=jax.ShapeDtypeStruct((n_seq, rows, HEAD_DIM), F32),
        grid_spec=grid_spec,
        compiler_params=pltpu.CompilerParams(
            dimension_semantics=("arbitrary", "arbitrary"), vmem_limit_bytes=VMEM_LIMIT_BYTES),
        name="moba_sample_attn",
    )(page_table, qbd, k_new, v_new, ownbias,
      *([cache_kt] * PAGES_PER_STEP), *([cache_vt] * PAGES_PER_STEP))


def _post_kernel(x_ref, a_ref, b_ref, gpre_ref, wg_ref, wpa_ref, wpb_ref, wout_ref,
                 gpm_ref, gpf_ref, wup_ref, wdown_ref, gff_ref, o_ref):
    x = x_ref[...]
    d_model = x.shape[1]
    h = _rmsnorm(x, gpre_ref[...]).astype(BF16)
    gates = jax.nn.sigmoid(_dot(h, wg_ref[...]))
    merged = (gates[:, :d_model] * _dot(a_ref[...], wpa_ref[...])
              + gates[:, d_model:] * _dot(b_ref[...], wpb_ref[...]))
    x = x + _rmsnorm(_dot(merged.astype(BF16), wout_ref[...]), gpm_ref[...])
    h2 = _rmsnorm(x, gpf_ref[...]).astype(BF16)
    slab = wup_ref.shape[1] // FFN_SPLIT
    f = jnp.zeros(x.shape, F32)
    for c in range(FFN_SPLIT):
        up = _dot(h2, wup_ref[:, c * slab:(c + 1) * slab])
        f = f + _dot(jnp.square(jnp.maximum(up, 0.0)).astype(BF16), wdown_ref[c * slab:(c + 1) * slab, :])
    o_ref[...] = x + _rmsnorm(f, gff_ref[...])


def _post(x, a, b, g_pre, w_g, w_pa, w_pb, w_out, g_pm, g_pf, w_up, w_down, g_ff, *, tm):
    rows, d_model = x.shape
    consts = (g_pre, w_g, w_pa, w_pb, w_out, g_pm, g_pf, w_up, w_down, g_ff)
    return pl.pallas_call(
        _post_kernel,
        out_shape=jax.ShapeDtypeStruct((rows, d_model), F32),
        grid=(rows // tm,),
        in_specs=[_row_spec(tm, d_model), _row_spec(tm, SGU_WIDTH), _row_spec(tm, ATTN_WIDTH)]
        + [_const_spec(c.shape) for c in consts],
        out_specs=_row_spec(tm, d_model),
        compiler_params=pltpu.CompilerParams(
            dimension_semantics=("arbitrary",), vmem_limit_bytes=VMEM_LIMIT_BYTES),
        name="merge_mlp",
    )(x, a, b, *consts)


def _pair_weights(w):
    g, p, j = w.shape
    return w.reshape(g // 2, 2, p, j).transpose(0, 2, 1, 3).reshape(g // 2, p, 2 * j)


def kernel(x_prompt, x_sample, cache_k, cache_v, page_table, g_pre_mix, w_in, g_sgu, w_s, b_s, w_pa, w_pb,
           w_out, g_post_mix, g_pre_ffn, w_up, w_down, g_post_ffn):
    depth = w_in.shape[0]
    batch, seq, d_model = x_prompt.shape
    dec_batch, dec_seq, _ = x_sample.shape
    assert batch == 1 and seq % KV_TILE == 0 and KV_TILE % PROMPT_ROWS == 0 and seq // MOBA_BLOCK <= HALF
    assert dec_seq <= CHUNK and SGU_GROUP_DIM == HALF and HEAD_DIM == HALF
    n_qkvuz = 3 * ATTN_WIDTH + 2 * SGU_WIDTH
    dec_rows = dec_batch * dec_seq

    cache_kt = cache_k.transpose(0, 1, 3, 4, 2)
    cache_vt = cache_v.transpose(0, 1, 3, 4, 2)
    xp = x_prompt.reshape(seq, d_model)
    xs = x_sample.reshape(dec_rows, d_model)
    kp_rows, vp_rows, ks_rows, vs_rows, zs_rows = [], [], [], [], []
    for l in range(depth):
        row = lambda g: g[l][None, :]
        w_qkvuz = w_in[l][:, :n_qkvuz].astype(BF16)
        w_gates = w_in[l][:, n_qkvuz:].astype(BF16)
        post_w = (row(g_pre_mix), w_gates, w_pa[l].astype(BF16), w_pb[l].astype(BF16), w_out[l].astype(BF16),
                  row(g_post_mix), row(g_pre_ffn), w_up[l].astype(BF16), w_down[l].astype(BF16), row(g_post_ffn))

        bias_p = jnp.repeat(b_s[l].T, SGU_GROUP_DIM, axis=1)
        qt, a, kmean, kaug, kt, vt, vtb = _inproj(
            xp, row(g_pre_mix), w_qkvuz, row(g_sgu), _pair_weights(w_s[l]), bias_p,
            tm=PROMPT_ROWS, chunk=CHUNK, emit_attn=True)
        qaug_t = _select(qt, _gate_matrices(kmean.reshape(seq // MOBA_BLOCK, ATTN_WIDTH)))
        b_attn = _moba_attn(qaug_t, kaug, vtb)
        xp = _post(xp, a, b_attn, *post_w, tm=PROMPT_ROWS)
        kp_rows.append(kt.reshape(N_HEADS, HEAD_DIM, seq).transpose(2, 0, 1)[None])
        vp_rows.append(vt.reshape(N_HEADS, HEAD_DIM, seq).transpose(2, 0, 1)[None])

        pos = jnp.arange(dec_rows) % dec_seq
        seq_id = jnp.arange(dec_rows) // dec_seq
        spread = jax.nn.one_hot(pos, dec_seq, dtype=F32)
        w_tiled = jnp.einsum("rp,gpj,cj->grc", spread, w_s[l][:, :dec_seq, :dec_seq], spread,
                             precision=lax.Precision.HIGHEST)
        w_blockdiag = jnp.where(seq_id[:, None] == seq_id[None, :], w_tiled, 0.0)
        bias_s = jnp.tile(bias_p[:dec_seq], (dec_batch, 1))
        q, a, k, v, z = _inproj(
            xs, row(g_pre_mix), w_qkvuz, row(g_sgu), _pair_weights(w_blockdiag), bias_s,
            tm=dec_rows, chunk=dec_rows, emit_attn=False)
        o = _sample_attn(q.reshape(dec_batch, dec_seq, ATTN_WIDTH), k.reshape(dec_batch, dec_seq, ATTN_WIDTH),
                         v.reshape(dec_batch, dec_seq, ATTN_WIDTH), cache_kt, cache_vt, page_table, l)
        b_attn = o.reshape(dec_batch, N_HEADS, dec_seq, HEAD_DIM).transpose(0, 2, 1, 3).reshape(
            dec_rows, ATTN_WIDTH).astype(BF16)
        xs = _post(xs, a, b_attn, *post_w, tm=dec_rows)
        ks_rows.append(k.reshape(dec_batch, dec_seq, N_HEADS, HEAD_DIM))
        vs_rows.append(v.reshape(dec_batch, dec_seq, N_HEADS, HEAD_DIM))
        zs_rows.append(z.reshape(dec_batch, dec_seq, SGU_WIDTH))

    return (xp.reshape(batch, seq, d_model), xs.reshape(dec_batch, dec_seq, d_model),
            jnp.stack(kp_rows), jnp.stack(vp_rows), jnp.stack(ks_rows), jnp.stack(vs_rows), jnp.stack(zs_rows))
```
